```python
import jax, jax.numpy as jnp
from jax import lax
import numpy as np

D_MODEL = 1024
BATCH = 2
SEQ = 8192
DEPTH = 2

HEAD_DIM = 64
BLOCK = 128
N_MEM = 256
BRANCH_WIDTH = 256
N_BRANCH = 4
POOL_WINDOWS = (2, 4, 8, 16)
POOL_GROUPS = 4
POOL_GDIM = BRANCH_WIDTH // POOL_GROUPS
DSA_HEADS = 4
DSA_LATENT = 128
IDX_HEADS = 8
IDX_DIM = 32
DSA_TOPK_MAX = 256
SB_HEADS = 4
SWA_HEADS = 4
SWA_KV_HEADS = 2
SWA_WINDOW = 128
MEM_HEADS = 4
D_FF = ((8 * D_MODEL // 3 + 255) // 256) * 256
RMS_EPS = 1e-6

SPLIT_SIZES = (
    BRANCH_WIDTH,
    DSA_HEADS * HEAD_DIM,
    DSA_LATENT,
    IDX_HEADS * IDX_DIM,
    IDX_DIM,
    IDX_HEADS,
    SB_HEADS * HEAD_DIM,
    SB_HEADS * HEAD_DIM,
    SB_HEADS * HEAD_DIM,
    SWA_HEADS * HEAD_DIM,
    SWA_KV_HEADS * HEAD_DIM,
    SWA_KV_HEADS * HEAD_DIM,
    N_BRANCH * D_MODEL,
)
D_IN = sum(SPLIT_SIZES)

kernel_name = "hybrid_gated_pool_dsa_stickbreak_swa"


def _split_points():
    pts, acc = [], 0
    for s in SPLIT_SIZES[:-1]:
        acc += s
        pts.append(acc)
    return pts


def rmsnorm(x, g):
    x32 = x.astype(jnp.float32)
    y = x32 * lax.rsqrt(jnp.mean(x32 * x32, axis=-1, keepdims=True) + RMS_EPS)
    return (y * g).astype(x.dtype)


def swiglu(h, w_up, w_down):
    gate, up = jnp.split(h @ w_up, 2, axis=-1)
    return (jax.nn.silu(gate) * up) @ w_down


def _blocks(a, nb):
    return a.reshape(a.shape[0], nb, BLOCK, *a.shape[2:]).swapaxes(0, 1)


def _unblocks(a):
    a = a.swapaxes(0, 1)
    return a.reshape(a.shape[0], a.shape[1] * a.shape[2], *a.shape[3:])


def _trailing_mean_minus_self(u, w):
    S = u.shape[1]
    c = jnp.cumsum(u, axis=1)
    c_prev = jnp.pad(c, ((0, 0), (w, 0), (0, 0)))[:, :S]
    cnt = jnp.minimum(jnp.arange(1, S + 1), w).astype(u.dtype)[None, :, None]
    return (c - c_prev) / cnt - u


def pool_mixer(u, pool_w, pool_scale):
    B, S, _ = u.shape
    ug = u.astype(jnp.float32).reshape(B, S, POOL_GROUPS, POOL_GDIM)
    pooled = jnp.stack([_trailing_mean_minus_self(ug[:, :, g], w)
                        for g, w in enumerate(POOL_WINDOWS)], axis=2)
    mixed = jnp.einsum('bsgc,gcd->bsgd', pooled.astype(u.dtype), pool_w)
    return mixed.reshape(B, S, BRANCH_WIDTH) * pool_scale


def dsa_mixer(q, c_kv, idx_q, idx_k, idx_w, kv_norm, w_uk, w_uv):
    B, S = q.shape[:2]
    nb = S // BLOCK
    topk = min(DSA_TOPK_MAX, S // 4)
    c = rmsnorm(c_kv, kv_norm)
    q_lat = jnp.einsum('bshd,rhd->bshr', q, w_uk)
    ik = idx_k.astype(jnp.float32)
    key_pos = jnp.arange(S)

    def one_block(args):
        n, qb, iqb, iwb = args
        qpos = n * BLOCK + jnp.arange(BLOCK)
        causal = key_pos[None, :] <= qpos[:, None]
        logits = jnp.einsum('bqhd,bsd->bqhs', iqb.astype(jnp.float32), ik) * IDX_DIM ** -0.5
        score = jnp.einsum('bqh,bqhs->bqs', iwb.astype(jnp.float32), jax.nn.relu(logits))
        score = jnp.where(causal[None], score, -jnp.inf)
        _, idx = lax.top_k(score, topk)
        valid = idx <= qpos[None, :, None]
        c_sel = jax.vmap(lambda cb, ib: cb[ib])(c, idx)
        s = jnp.einsum('bqhr,bqkr->bqhk', qb.astype(jnp.float32),
                       c_sel.astype(jnp.float32)) * HEAD_DIM ** -0.5
        s = jnp.where(valid[:, :, None, :], s, -jnp.inf)
        p = jax.nn.softmax(s, axis=-1)
        return jnp.einsum('bqhk,bqkr->bqhr', p.astype(c.dtype), c_sel)

    o_lat = lax.map(one_block, (jnp.arange(nb), _blocks(q_lat, nb),
                                _blocks(idx_q, nb), _blocks(idx_w, nb)))
    o = jnp.einsum('bshr,rhd->bshd', _unblocks(o_lat), w_uv)
    return o.reshape(B, S, DSA_HEADS * HEAD_DIM)


def stick_breaking_mixer(q, k, v):
    B, S, H, dh = q.shape
    nb = S // BLOCK
    kf = k.astype(jnp.float32)
    key_pos = jnp.arange(S)

    def one_block(args):
        n, qb = args
        qpos = n * BLOCK + jnp.arange(BLOCK)
        strict = (key_pos[None, :] < qpos[:, None])[None, None]
        z = jnp.einsum('bqhd,bshd->bhqs', qb.astype(jnp.float32), kf) * dh ** -0.5
        log_1m_beta = jnp.where(strict, jax.nn.log_sigmoid(-z), 0.0)
        between = lax.cumsum(log_1m_beta, axis=3, reverse=True) - log_1m_beta
        a = jnp.where(strict, jnp.exp(jax.nn.log_sigmoid(z) + between), 0.0)
        return jnp.einsum('bhqs,bshd->bqhd', a.astype(v.dtype), v)

    o = lax.map(one_block, (jnp.arange(nb), _blocks(q, nb)))
    return _unblocks(o).reshape(B, S, H * dh)


def swa_sink_mixer(q, k, v, sinks):
    B, S, HQ, dh = q.shape
    HKV = k.shape[2]
    G = HQ // HKV
    nb = S // BLOCK
    qb = q.reshape(B, nb, BLOCK, HKV, G, dh)

    def band(a):
        ab = a.reshape(B, nb, BLOCK, HKV, dh)
        prev = jnp.pad(ab, ((0, 0), (1, 0), (0, 0), (0, 0), (0, 0)))[:, :nb]
        return jnp.concatenate([prev, ab], axis=2)

    kb, vb = band(k), band(v)
    s = jnp.einsum('bnqkgd,bnjkd->bnkgqj', qb.astype(jnp.float32),
                   kb.astype(jnp.float32)) * dh ** -0.5
    kj = jnp.arange(2 * BLOCK)[None, :]
    rel = (jnp.arange(BLOCK)[:, None] + BLOCK) - kj
    in_window = (rel >= 0) & (rel < SWA_WINDOW)
    key_ok = (jnp.arange(nb)[:, None] * BLOCK - BLOCK + kj) >= 0
    mask = in_window[None] & key_ok[:, None, :]
    s = jnp.where(mask[None, :, None, None], s, -jnp.inf)
    sink = jnp.broadcast_to(sinks.astype(jnp.float32).reshape(HKV, G)[None, None, :, :, None, None],
                            s.shape[:-1] + (1,))
    p = jax.nn.softmax(jnp.concatenate([s, sink], axis=-1), axis=-1)[..., :-1]
    o = jnp.einsum('bnkgqj,bnjkd->bnqkgd', p.astype(v.dtype), vb)
    return o.reshape(B, S, HQ * dh)


def hybrid_mixer(h, w_in, pool_w, pool_scale, dsa_kv_norm, dsa_w_uk, dsa_w_uv,
                 swa_sinks, w_branch, w_out):
    B, S, _ = h.shape
    (u_pool, dsa_q, dsa_ckv, idx_q, idx_k, idx_w, sb_q, sb_k, sb_v,
     swa_q, swa_k, swa_v, gate_logits) = jnp.split(h @ w_in, _split_points(), axis=-1)
    o_a = pool_mixer(u_pool, pool_w, pool_scale)
    o_b = dsa_mixer(dsa_q.reshape(B, S, DSA_HEADS, HEAD_DIM), dsa_ckv,
                    idx_q.reshape(B, S, IDX_HEADS, IDX_DIM), idx_k,
                    idx_w * IDX_HEADS ** -0.5, dsa_kv_norm, dsa_w_uk, dsa_w_uv)
    o_c = stick_breaking_mixer(sb_q.reshape(B, S, SB_HEADS, HEAD_DIM),
                               sb_k.reshape(B, S, SB_HEADS, HEAD_DIM),
                               sb_v.reshape(B, S, SB_HEADS, HEAD_DIM))
    o_d = swa_sink_mixer(swa_q.reshape(B, S, SWA_HEADS, HEAD_DIM),
                         swa_k.reshape(B, S, SWA_KV_HEADS, HEAD_DIM),
                         swa_v.reshape(B, S, SWA_KV_HEADS, HEAD_DIM), swa_sinks)
    branches = jnp.stack([o_a, o_b, o_c, o_d], axis=2)
    y = jnp.einsum('bsic,icd->bsid', branches, w_branch)
    gates = jax.nn.sigmoid(gate_logits.reshape(B, S, N_BRANCH, D_MODEL))
    merged = jnp.sum(gates * y, axis=2)
    return merged @ w_out


def memory_cross_attention(h, mem_n, w_q, w_kv, w_o):
    B, S, _ = h.shape
    M = mem_n.shape[1]
    q = (h @ w_q).reshape(B, S, MEM_HEADS, HEAD_DIM)
    k, v = jnp.split(mem_n @ w_kv, 2, axis=-1)
    k = k.reshape(B, M, MEM_HEADS, HEAD_DIM)
    v = v.reshape(B, M, MEM_HEADS, HEAD_DIM)
    s = jnp.einsum('bshd,bmhd->bhsm', q.astype(jnp.float32), k.astype(jnp.float32)) * HEAD_DIM ** -0.5
    p = jax.nn.softmax(s, axis=-1)
    o = jnp.einsum('bhsm,bmhd->bshd', p.astype(v.dtype), v)
    return o.reshape(B, S, MEM_HEADS * HEAD_DIM) @ w_o


def setup_inputs(seed: int = 0) -> dict:
    key = jax.random.key(seed)
    ks = iter(jax.random.split(key, 40))

    def nrm(shape, scale):
        return jax.random.normal(next(ks), shape, jnp.float32) * scale

    def gain(shape):
        return 1.0 + nrm(shape, 0.05)

    L, D, F = DEPTH, D_MODEL, D_FF
    return {
        "x": nrm((BATCH, SEQ, D), 1.0),
        "mem": nrm((BATCH, N_MEM, D), 1.0),
        "ffn1_pre": gain((L, D)),
        "ffn1_w_up": nrm((L, D, 2 * F), D ** -0.5),
        "ffn1_w_down": nrm((L, F, D), F ** -0.5),
        "ffn1_post": gain((L, D)),
        "mix_pre": gain((L, D)),
        "w_in": nrm((L, D, D_IN), D ** -0.5),
        "pool_w": nrm((L, POOL_GROUPS, POOL_GDIM, POOL_GDIM), POOL_GDIM ** -0.5),
        "pool_scale": 1.0 + nrm((L, BRANCH_WIDTH), 0.1),
        "dsa_kv_norm": gain((L, DSA_LATENT)),
        "dsa_w_uk": nrm((L, DSA_LATENT, DSA_HEADS, HEAD_DIM), HEAD_DIM ** -0.5),
        "dsa_w_uv": nrm((L, DSA_LATENT, DSA_HEADS, HEAD_DIM), DSA_LATENT ** -0.5),
        "swa_sinks": nrm((L, SWA_HEADS), 0.5),
        "w_branch": nrm((L, N_BRANCH, BRANCH_WIDTH, D), BRANCH_WIDTH ** -0.5),
        "w_out": nrm((L, D, D), D ** -0.5),
        "mix_post": gain((L, D)),
        "xattn_pre": gain((L, D)),
        "mem_norm": gain((L, D)),
        "xattn_w_q": nrm((L, D, MEM_HEADS * HEAD_DIM), D ** -0.5),
        "xattn_w_kv": nrm((L, D, 2 * MEM_HEADS * HEAD_DIM), D ** -0.5),
        "xattn_w_o": nrm((L, MEM_HEADS * HEAD_DIM, D), (MEM_HEADS * HEAD_DIM) ** -0.5),
        "xattn_post": gain((L, D)),
        "ffn2_pre": gain((L, D)),
        "ffn2_w_up": nrm((L, D, 2 * F), D ** -0.5),
        "ffn2_w_down": nrm((L, F, D), F ** -0.5),
        "ffn2_post": gain((L, D)),
    }


def reference(x, mem, ffn1_pre, ffn1_w_up, ffn1_w_down, ffn1_post,
              mix_pre, w_in, pool_w, pool_scale, dsa_kv_norm, dsa_w_uk, dsa_w_uv,
              swa_sinks, w_branch, w_out, mix_post,
              xattn_pre, mem_norm, xattn_w_q, xattn_w_kv, xattn_w_o, xattn_post,
              ffn2_pre, ffn2_w_up, ffn2_w_down, ffn2_post):
    for l in range(DEPTH):
        x = x + 0.5 * rmsnorm(swiglu(rmsnorm(x, ffn1_pre[l]), ffn1_w_up[l], ffn1_w_down[l]),
                              ffn1_post[l])
        m = hybrid_mixer(rmsnorm(x, mix_pre[l]), w_in[l], pool_w[l], pool_scale[l],
                         dsa_kv_norm[l], dsa_w_uk[l], dsa_w_uv[l], swa_sinks[l],
                         w_branch[l], w_out[l])
        x = x + rmsnorm(m, mix_post[l])
        c = memory_cross_attention(rmsnorm(x, xattn_pre[l]), rmsnorm(mem, mem_norm[l]),
                                   xattn_w_q[l], xattn_w_kv[l], xattn_w_o[l])
        x = x + rmsnorm(c, xattn_post[l])
        x = x + 0.5 * rmsnorm(swiglu(rmsnorm(x, ffn2_pre[l]), ffn2_w_up[l], ffn2_w_down[l]),
                              ffn2_post[l])
    return x
```

```python
import functools

import jax
import jax.numpy as jnp
from jax import lax
from jax.experimental import pallas as pl
from jax.experimental.pallas import tpu as pltpu

F32 = jnp.float32
BF16 = jnp.bfloat16
I32 = jnp.int32

RMS_EPS = 1e-6
HEAD_DIM = 64
BRANCH_WIDTH = 256
N_BRANCH = 4
POOL_WINDOWS = (2, 4, 8, 16)
POOL_HALO = 16
DSA_HEADS = 4
DSA_LATENT = 128
IDX_HEADS = 8
IDX_DIM = 32
DSA_TOPK_MAX = 256
SB_HEADS = 4
SWA_HEADS = 4
SWA_KV_HEADS = 2
SWA_WINDOW = 128
MEM_HEADS = 4

BLK = 128
VMEM_LIMIT = 48 * 1024 * 1024
MASKED = -1e30
INT_MIN = -(2 ** 31)
KEY_NEG_INF = (0xFF800000 - (1 << 32)) ^ 0x7FFFFFFF
EXP_ZERO_BELOW = -104.0


def _cparams(sem):
    return pltpu.CompilerParams(dimension_semantics=sem, vmem_limit_bytes=VMEM_LIMIT)


def _rms(x, g):
    return x * lax.rsqrt(jnp.mean(x * x, axis=-1, keepdims=True) + RMS_EPS) * g


def _const_spec(shape):
    nd = len(shape)
    return pl.BlockSpec(shape, lambda *_: (0,) * nd)


def _head_stack(q, heads):
    rows, width = q.shape
    per = width // heads
    lane = lax.broadcasted_iota(I32, (rows, width), 1)
    zero = jnp.zeros_like(q)
    return jnp.concatenate(
        [jnp.where((lane >= h * per) & (lane < (h + 1) * per), q, zero) for h in range(heads)], axis=0)


def _head_merge(o, heads):
    rows = o.shape[0] // heads
    width = o.shape[1]
    per = width // heads
    lane = lax.broadcasted_iota(I32, (rows, width), 1)
    out = jnp.zeros((rows, width), o.dtype)
    for h in range(heads):
        out = jnp.where((lane >= h * per) & (lane < (h + 1) * per), o[h * rows:(h + 1) * rows], out)
    return out


def _ffn_kernel(x_ref, pre_ref, wg_ref, wu_ref, wd_ref, post_ref, o_ref, hn_ref, acc_ref):
    j = pl.program_id(1)

    @pl.when(j == 0)
    def _():
        hn_ref[...] = _rms(x_ref[...], pre_ref[...]).astype(BF16)
        acc_ref[...] = jnp.zeros_like(acc_ref)

    hn = hn_ref[...]
    g = jnp.dot(hn, wg_ref[...], preferred_element_type=F32)
    u = jnp.dot(hn, wu_ref[...], preferred_element_type=F32)
    a = (g * jax.nn.sigmoid(g) * u).astype(BF16)
    acc_ref[...] += jnp.dot(a, wd_ref[...], preferred_element_type=F32)

    @pl.when(j == pl.num_programs(1) - 1)
    def _():
        o_ref[...] = x_ref[...] + 0.5 * _rms(acc_ref[...], post_ref[...])


def _ffn(x2, pre, w_up, w_down, post, *, tm, tf):
    T, D = x2.shape
    F = w_down.shape[0]
    nf = F // tf
    return pl.pallas_call(
        _ffn_kernel,
        out_shape=jax.ShapeDtypeStruct((T, D), F32),
        grid=(T // tm, nf),
        in_specs=[
            pl.BlockSpec((tm, D), lambda i, j: (i, 0)),
            pl.BlockSpec((1, D), lambda i, j: (0, 0)),
            pl.BlockSpec((D, tf), lambda i, j: (0, j)),
            pl.BlockSpec((D, tf), lambda i, j: (0, j + nf)),
            pl.BlockSpec((tf, D), lambda i, j: (j, 0)),
            pl.BlockSpec((1, D), lambda i, j: (0, 0)),
        ],
        out_specs=pl.BlockSpec((tm, D), lambda i, j: (i, 0)),
        scratch_shapes=[pltpu.VMEM((tm, D), BF16), pltpu.VMEM((tm, D), F32)],
        compiler_params=_cparams(("arbitrary", "arbitrary")),
        name="ffn",
    )(x2, pre, w_up, w_up, w_down, post)


_SEG = dict(pool=(0, 256), dq=(256, 512), ckv=(512, 640), iq=(640, 896), ikw=(896, 1024),
            sbq=(1024, 1280), sbk=(1280, 1536), sbv=(1536, 1792),
            swq=(1792, 2048), swk=(2048, 2304), swv=(2304, 2560))
_PROJ_W = 2560


def _proj_kernel(x_ref, pre_ref, w_ref, poolw_ref, pools_ref, kvn_ref, wuk_ref,
                 oa_ref, qlat_ref, c_ref, cT_ref, iq_ref, iw_ref, ikT_ref,
                 sbq_ref, sbkT_ref, sbv_ref, swq_ref, swkT_ref, swv_ref, halo_ref):
    si = pl.program_id(1)
    tm = x_ref.shape[1]
    nb = tm // BLK
    hn = _rms(x_ref[0], pre_ref[...]).astype(BF16)
    p = jnp.dot(hn, w_ref[...], preferred_element_type=F32)

    def seg(name):
        a, b = _SEG[name]
        return p[:, a:b]

    @pl.when(si == 0)
    def _():
        halo_ref[...] = jnp.zeros_like(halo_ref)

    u = seg("pool")
    s1 = jnp.concatenate([halo_ref[...], u], axis=0)
    halo_ref[...] = u[tm - POOL_HALO:, :]
    s2 = s1[1:] + s1[:-1]
    s4 = s2[2:] + s2[:-2]
    s8 = s4[4:] + s4[:-4]
    s16 = s8[8:] + s8[:-8]
    lane = lax.broadcasted_iota(I32, (tm, BRANCH_WIDTH), 1)
    t1 = (lax.broadcasted_iota(I32, (tm, BRANCH_WIDTH), 0) + si * tm + 1).astype(F32)
    gw = BRANCH_WIDTH // len(POOL_WINDOWS)
    wsum = jnp.where(lane < gw, s2[-tm:], jnp.where(lane < 2 * gw, s4[-tm:],
                     jnp.where(lane < 3 * gw, s8[-tm:], s16[-tm:])))
    win = jnp.where(lane < gw, float(POOL_WINDOWS[0]), jnp.where(lane < 2 * gw, float(POOL_WINDOWS[1]),
                    jnp.where(lane < 3 * gw, float(POOL_WINDOWS[2]), float(POOL_WINDOWS[3]))))
    pooled = wsum / jnp.minimum(t1, win) - u
    mixed = jnp.dot(pooled.astype(BF16), poolw_ref[...], preferred_element_type=F32)
    oa_ref[0] = (mixed * pools_ref[...]).astype(BF16)

    qlat = jnp.dot(seg("dq").astype(BF16), wuk_ref[...], preferred_element_type=F32)
    qlat_ref[0] = (qlat * HEAD_DIM ** -0.5).astype(BF16)
    c = _rms(seg("ckv"), kvn_ref[...])
    c_ref[0] = c.astype(BF16)
    iq_ref[0] = (seg("iq") * IDX_DIM ** -0.5).astype(BF16)
    ikw = seg("ikw")
    iw_ref[0] = ikw * IDX_HEADS ** -0.5
    sbq_ref[0] = (seg("sbq") * HEAD_DIM ** -0.5).astype(BF16)
    sbv_ref[0] = seg("sbv").astype(BF16)
    swq_ref[0] = (seg("swq") * HEAD_DIM ** -0.5).astype(BF16)
    swv_ref[0] = seg("swv").astype(BF16)
    sbk = seg("sbk")
    swk = seg("swk")
    for b in range(nb):
        rows = slice(b * BLK, (b + 1) * BLK)
        cT_ref[0, b] = c[rows].T.astype(BF16)
        ikT = ikw[rows].T[:IDX_DIM].astype(BF16)
        ikT_ref[0, b] = jnp.concatenate([ikT] * IDX_HEADS, axis=0)
        sbkT_ref[0, b] = sbk[rows].T.astype(BF16)
        swkT_ref[0, b] = swk[rows].T.astype(BF16)


def _proj(x, pre, w, poolw, pools, kvn, wuk, *, tm):
    B, S, D = x.shape
    nq = S // BLK
    nb = tm // BLK

    def rows(width, dtype):
        return jax.ShapeDtypeStruct((B, S, width), dtype), pl.BlockSpec((1, tm, width), lambda b, s: (b, s, 0))

    def blocks(height):
        return (jax.ShapeDtypeStruct((B, nq, height, BLK), BF16),
                pl.BlockSpec((1, nb, height, BLK), lambda b, s: (b, s, 0, 0)))

    outs = [rows(256, BF16),
            rows(512, BF16),
            rows(128, BF16),
            blocks(128),
            rows(256, BF16),
            rows(128, F32),
            blocks(256),
            rows(256, BF16),
            blocks(256),
            rows(256, BF16),
            rows(256, BF16),
            blocks(256),
            rows(256, BF16)]
    return pl.pallas_call(
        _proj_kernel,
        out_shape=[o[0] for o in outs],
        grid=(B, S // tm),
        in_specs=[pl.BlockSpec((1, tm, D), lambda b, s: (b, s, 0)),
                  _const_spec((1, D)), _const_spec(w.shape), _const_spec(poolw.shape),
                  _const_spec(pools.shape), _const_spec(kvn.shape), _const_spec(wuk.shape)],
        out_specs=[o[1] for o in outs],
        scratch_shapes=[pltpu.VMEM((POOL_HALO, BRANCH_WIDTH), F32)],
        compiler_params=_cparams(("arbitrary", "arbitrary")),
        name="proj",
    )(x, pre, w, poolw, pools, kvn, wuk)


def _sort_key(s):
    b = pltpu.bitcast(s, I32)
    return b ^ ((b >> 31) & 0x7FFFFFFF)


def _dsa_kernel(qlat_ref, iq_ref, iw_ref, ikT_ref, cT_ref, c_ref, wuv_ref, o_ref,
                key_ref, m_ref, l_ref, acc_ref, *, topk):
    n = pl.program_id(1)
    kf = float(topk)
    row = lax.broadcasted_iota(I32, (BLK, BLK), 0)
    col = lax.broadcasted_iota(I32, (BLK, BLK), 1)

    iqm = _head_stack(iq_ref[0], IDX_HEADS)
    iw = iw_ref[0]
    iwb = [jnp.broadcast_to(iw[:, IDX_DIM + h:IDX_DIM + h + 1], (BLK, BLK)) for h in range(IDX_HEADS)]

    def score_block(j):
        lg = jnp.dot(iqm, ikT_ref[0, j], preferred_element_type=F32)
        acc = iwb[0] * jnp.maximum(lg[0:BLK], 0.0)
        for h in range(1, IDX_HEADS):
            acc = acc + iwb[h] * jnp.maximum(lg[h * BLK:(h + 1) * BLK], 0.0)
        return acc

    def score_body(j, carry):
        key_ref[j] = _sort_key(score_block(j))
        return carry

    lax.fori_loop(0, n, score_body, 0)
    key_ref[n] = _sort_key(jnp.where(col <= row, score_block(n), -jnp.inf))

    def count_ge(cand):
        cb = jnp.broadcast_to(cand, (BLK, BLK))

        def body(j, a):
            return a + jnp.where(key_ref[j] >= cb, 1.0, 0.0)

        a = lax.fori_loop(0, n + 1, body, jnp.zeros((BLK, BLK), F32))
        return jnp.sum(a, axis=1, keepdims=True)

    total = jnp.full((BLK, 1), 1.0, F32) * ((n + 1) * BLK).astype(F32)
    cnt0 = count_ge(jnp.zeros((BLK, 1), I32))
    ok0 = cnt0 >= kf
    thr = jnp.where(ok0, 0, INT_MIN).astype(I32)
    cnt_thr = jnp.where(ok0, cnt0, total)

    def bisect(i, carry):
        t, ct = carry
        cand = t + jnp.left_shift(jnp.int32(1), 30 - i)
        cnt = count_ge(cand)
        ok = cnt >= kf
        return jnp.where(ok, cand, t), jnp.where(ok, cnt, ct)

    thr, cnt_thr = lax.fori_loop(0, 31, bisect, (thr, cnt_thr))
    cnt_gt = count_ge(thr + 1)
    need = kf - cnt_gt
    n_eq = cnt_thr - cnt_gt

    @pl.when(jnp.max(n_eq - need) > 0.0)
    def _():
        tb = jnp.broadcast_to(thr, (BLK, BLK))
        incl = jnp.where(row <= col, 1.0, 0.0).astype(BF16)

        def body(j, run):
            k = key_ref[j]
            eq = k == tb
            eqf = jnp.where(eq, 1.0, 0.0)
            pref = jnp.dot(eqf.astype(BF16), incl, preferred_element_type=F32)
            keep = (run + pref) <= need
            key_ref[j] = jnp.where(eq & jnp.logical_not(keep), INT_MIN, k)
            return run + jnp.sum(eqf, axis=1, keepdims=True)

        lax.fori_loop(0, n + 1, body, jnp.zeros((BLK, 1), F32))

    tsel = jnp.broadcast_to(jnp.maximum(thr, KEY_NEG_INF + 1), (BLK, BLK))

    qlat = qlat_ref[0]
    qs = jnp.concatenate([qlat[:, h * DSA_LATENT:(h + 1) * DSA_LATENT] for h in range(DSA_HEADS)], axis=0)
    m_ref[...] = jnp.full(m_ref.shape, MASKED, F32)
    l_ref[...] = jnp.zeros_like(l_ref)
    acc_ref[...] = jnp.zeros_like(acc_ref)

    def attn_body(j, carry):
        s = jnp.dot(qs, cT_ref[0, j], preferred_element_type=F32)
        sel = key_ref[j] >= tsel
        s = jnp.where(jnp.concatenate([sel] * DSA_HEADS, axis=0), s, MASKED)
        m_old = m_ref[...]
        m_new = jnp.maximum(m_old, jnp.max(s, axis=1, keepdims=True))
        alpha = jnp.exp(m_old - m_new)
        p = jnp.exp(s - m_new)
        l_ref[...] = alpha * l_ref[...] + jnp.sum(p, axis=1, keepdims=True)
        cj = c_ref[0, pl.ds(pl.multiple_of(j * BLK, BLK), BLK), :]
        acc_ref[...] = alpha * acc_ref[...] + jnp.dot(p.astype(BF16), cj, preferred_element_type=F32)
        m_ref[...] = m_new
        return carry

    lax.fori_loop(0, n + 1, attn_body, 0)
    olat = (acc_ref[...] / l_ref[...]).astype(BF16)
    out = jnp.dot(olat[0:BLK], wuv_ref[0], preferred_element_type=F32)
    for h in range(1, DSA_HEADS):
        out = out + jnp.dot(olat[h * BLK:(h + 1) * BLK], wuv_ref[h], preferred_element_type=F32)
    o_ref[0] = out.astype(BF16)


def _dsa(qlat, iq, iw, ikT, cT, c, wuv):
    B, S, _ = qlat.shape
    nq = S // BLK
    topk = min(DSA_TOPK_MAX, S // 4)
    return pl.pallas_call(
        functools.partial(_dsa_kernel, topk=topk),
        out_shape=jax.ShapeDtypeStruct((B, S, BRANCH_WIDTH), BF16),
        grid=(B, nq),
        in_specs=[pl.BlockSpec((1, BLK, qlat.shape[2]), lambda b, n: (b, n, 0)),
                  pl.BlockSpec((1, BLK, iq.shape[2]), lambda b, n: (b, n, 0)),
                  pl.BlockSpec((1, BLK, iw.shape[2]), lambda b, n: (b, n, 0)),
                  pl.BlockSpec((1,) + ikT.shape[1:], lambda b, n: (b, 0, 0, 0)),
                  pl.BlockSpec((1,) + cT.shape[1:], lambda b, n: (b, 0, 0, 0)),
                  pl.BlockSpec((1,) + c.shape[1:], lambda b, n: (b, 0, 0)),
                  _const_spec(wuv.shape)],
        out_specs=pl.BlockSpec((1, BLK, BRANCH_WIDTH), lambda b, n: (b, n, 0)),
        scratch_shapes=[pltpu.VMEM((nq, BLK, BLK), I32),
                        pltpu.VMEM((DSA_HEADS * BLK, 1), F32),
                        pltpu.VMEM((DSA_HEADS * BLK, 1), F32),
                        pltpu.VMEM((DSA_HEADS * BLK, DSA_LATENT), F32)],
        compiler_params=_cparams(("arbitrary", "arbitrary")),
        name="dsa",
    )(qlat, iq, iw, ikT, cT, c, wuv)


def _sb_kernel(q_ref, kT_ref, v_ref, o_ref, acc_ref):
    n = pl.program_id(1)
    H = SB_HEADS
    qm = _head_stack(q_ref[0], H)
    row = lax.broadcasted_iota(I32, (BLK, BLK), 0)
    col = lax.broadcasted_iota(I32, (BLK, BLK), 1)
    after = jnp.where(row > col, 1.0, 0.0).astype(BF16)
    strict = jnp.concatenate([col < row] * H, axis=0)

    def step(j, run, diag):
        z = jnp.dot(qm, kT_ref[0, j], preferred_element_type=F32)
        t = jnp.log1p(jnp.exp(-jnp.abs(z)))
        l1m = -(jnp.maximum(z, 0.0) + t)
        lsg = jnp.minimum(z, 0.0) - t
        if diag:
            l1m = jnp.where(strict, l1m, 0.0)
        hi = l1m.astype(BF16)
        r1 = l1m - hi.astype(F32)
        mid = r1.astype(BF16)
        lo = (r1 - mid.astype(F32)).astype(BF16)
        suf = (jnp.dot(hi, after, preferred_element_type=F32)
               + jnp.dot(mid, after, preferred_element_type=F32)
               + jnp.dot(lo, after, preferred_element_type=F32))
        a = jnp.exp(lsg + run + suf)
        if diag:
            a = jnp.where(strict, a, 0.0)
        vj = v_ref[0, pl.ds(pl.multiple_of(j * BLK, BLK), BLK), :]
        pv = jnp.dot(a.astype(BF16), vj, preferred_element_type=F32)
        if diag:
            acc_ref[...] = pv
        else:
            acc_ref[...] += pv
        return run + jnp.sum(l1m, axis=1, keepdims=True)

    run0 = step(n, jnp.zeros((H * BLK, 1), F32), True)

    def cond(carry):
        j, _, alive = carry
        return jnp.logical_and(j >= 0, alive)

    def body(carry):
        j, run, _ = carry
        run = step(j, run, False)
        return j - 1, run, jnp.max(run) > EXP_ZERO_BELOW

    lax.while_loop(cond, body, (n - 1, run0, jnp.max(run0) > EXP_ZERO_BELOW))
    o_ref[0] = _head_merge(acc_ref[...], H).astype(BF16)


def _sb(q, kT, v):
    B, S, W = q.shape
    nq = S // BLK
    return pl.pallas_call(
        _sb_kernel,
        out_shape=jax.ShapeDtypeStruct((B, S, W), BF16),
        grid=(B, nq),
        in_specs=[pl.BlockSpec((1, BLK, W), lambda b, n: (b, n, 0)),
                  pl.BlockSpec((1,) + kT.shape[1:], lambda b, n: (b, 0, 0, 0)),
                  pl.BlockSpec((1,) + v.shape[1:], lambda b, n: (b, 0, 0))],
        out_specs=pl.BlockSpec((1, BLK, W), lambda b, n: (b, n, 0)),
        scratch_shapes=[pltpu.VMEM((SB_HEADS * BLK, W), F32)],
        compiler_params=_cparams(("arbitrary", "arbitrary")),
        name="sb",
    )(q, kT, v)


def _swa_kernel(sink_ref, q_ref, kTp_ref, kTc_ref, vp_ref, vc_ref, o_ref):
    n = pl.program_id(1)
    H = SWA_HEADS
    qm = _head_stack(q_ref[0], H)
    kT = jnp.concatenate([kTp_ref[0, 0], kTc_ref[0, 0]], axis=1)
    s = jnp.dot(qm, kT, preferred_element_type=F32)
    r = lax.broadcasted_iota(I32, (BLK, 2 * BLK), 0)
    c = lax.broadcasted_iota(I32, (BLK, 2 * BLK), 1)
    rel = r + BLK - c
    ok = (rel >= 0) & (rel < SWA_WINDOW) & ((c >= BLK) | (n > 0))
    s = jnp.where(jnp.concatenate([ok] * H, axis=0), s, MASKED)
    sink = jnp.concatenate([jnp.full((BLK, 1), sink_ref[h], F32) for h in range(H)], axis=0)
    m = jnp.maximum(jnp.max(s, axis=1, keepdims=True), sink)
    p = jnp.exp(s - m)
    denom = jnp.sum(p, axis=1, keepdims=True) + jnp.exp(sink - m)
    p = (p / denom).astype(BF16)
    v = jnp.concatenate([vp_ref[0], vc_ref[0]], axis=0)
    o_ref[0] = _head_merge(jnp.dot(p, v, preferred_element_type=F32), H).astype(BF16)


def _swa(sinks, q, kT, v):
    B, S, W = q.shape
    nq = S // BLK
    prev = lambda b, n: (b, jnp.maximum(n - 1, 0), 0, 0)
    prev3 = lambda b, n: (b, jnp.maximum(n - 1, 0), 0)
    return pl.pallas_call(
        _swa_kernel,
        out_shape=jax.ShapeDtypeStruct((B, S, W), BF16),
        grid=(B, nq),
        in_specs=[pl.BlockSpec(memory_space=pltpu.SMEM),
                  pl.BlockSpec((1, BLK, W), lambda b, n: (b, n, 0)),
                  pl.BlockSpec((1, 1, W, BLK), prev),
                  pl.BlockSpec((1, 1, W, BLK), lambda b, n: (b, n, 0, 0)),
                  pl.BlockSpec((1, BLK, W), prev3),
                  pl.BlockSpec((1, BLK, W), lambda b, n: (b, n, 0))],
        out_specs=pl.BlockSpec((1, BLK, W), lambda b, n: (b, n, 0)),
        compiler_params=_cparams(("arbitrary", "arbitrary")),
        name="swa",
    )(sinks, q, kT, kT, v, v)


def _merge_kernel(x_ref, pre_ref, wg_ref, oa_ref, ob_ref, oc_ref, od_ref, wb_ref, wo_ref, post_ref, o_ref):
    x = x_ref[...]
    D = x.shape[1]
    hn = _rms(x, pre_ref[...]).astype(BF16)
    merged = None
    for i, br in enumerate((oa_ref, ob_ref, oc_ref, od_ref)):
        gate = jax.nn.sigmoid(jnp.dot(hn, wg_ref[:, i * D:(i + 1) * D], preferred_element_type=F32))
        y = jnp.dot(br[...], wb_ref[i], preferred_element_type=F32)
        merged = gate * y if merged is None else merged + gate * y
    m = jnp.dot(merged.astype(BF16), wo_ref[...], preferred_element_type=F32)
    o_ref[...] = x + _rms(m, post_ref[...])


def _merge(x2, pre, wg, oa, ob, oc, od, wb, wo, post, *, tm):
    T, D = x2.shape
    W = oa.shape[1]
    br = pl.BlockSpec((tm, W), lambda i: (i, 0))
    return pl.pallas_call(
        _merge_kernel,
        out_shape=jax.ShapeDtypeStruct((T, D), F32),
        grid=(T // tm,),
        in_specs=[pl.BlockSpec((tm, D), lambda i: (i, 0)), _const_spec((1, D)), _const_spec(wg.shape),
                  br, br, br, br, _const_spec(wb.shape), _const_spec(wo.shape), _const_spec((1, D))],
        out_specs=pl.BlockSpec((tm, D), lambda i: (i, 0)),
        compiler_params=_cparams(("arbitrary",)),
        name="merge",
    )(x2, pre, wg, oa, ob, oc, od, wb, wo, post)


def _memkv_kernel(mem_ref, g_ref, wkv_ref, kT_ref, v_ref):
    mn = _rms(mem_ref[0], g_ref[...]).astype(BF16)
    kv = jnp.dot(mn, wkv_ref[...], preferred_element_type=F32)
    W = kv.shape[1] // 2
    kT_ref[0] = kv[:, :W].T.astype(BF16)
    v_ref[0] = kv[:, W:].astype(BF16)


def _memkv(mem, g, wkv):
    B, M, D = mem.shape
    W = wkv.shape[1] // 2
    return pl.pallas_call(
        _memkv_kernel,
        out_shape=[jax.ShapeDtypeStruct((B, W, M), BF16), jax.ShapeDtypeStruct((B, M, W), BF16)],
        grid=(B,),
        in_specs=[pl.BlockSpec((1, M, D), lambda b: (b, 0, 0)), _const_spec((1, D)), _const_spec(wkv.shape)],
        out_specs=[pl.BlockSpec((1, W, M), lambda b: (b, 0, 0)), pl.BlockSpec((1, M, W), lambda b: (b, 0, 0))],
        compiler_params=_cparams(("arbitrary",)),
        name="memkv",
    )(mem, g, wkv)


def _xattn_kernel(x_ref, pre_ref, wq_ref, kT_ref, v_ref, wo_ref, post_ref, o_ref):
    x = x_ref[0]
    hn = _rms(x, pre_ref[...]).astype(BF16)
    q = (jnp.dot(hn, wq_ref[...], preferred_element_type=F32) * HEAD_DIM ** -0.5).astype(BF16)
    s = jnp.dot(_head_stack(q, MEM_HEADS), kT_ref[0], preferred_element_type=F32)
    p = jnp.exp(s - jnp.max(s, axis=1, keepdims=True))
    p = (p / jnp.sum(p, axis=1, keepdims=True)).astype(BF16)
    o = _head_merge(jnp.dot(p, v_ref[0], preferred_element_type=F32), MEM_HEADS)
    c = jnp.dot(o.astype(BF16), wo_ref[...], preferred_element_type=F32)
    o_ref[0] = x + _rms(c, post_ref[...])


def _xattn(x, pre, wq, kT, v, wo, post, *, tm):
    B, S, D = x.shape
    return pl.pallas_call(
        _xattn_kernel,
        out_shape=jax.ShapeDtypeStruct((B, S, D), F32),
        grid=(B, S // tm),
        in_specs=[pl.BlockSpec((1, tm, D), lambda b, s: (b, s, 0)), _const_spec((1, D)), _const_spec(wq.shape),
                  pl.BlockSpec((1,) + kT.shape[1:], lambda b, s: (b, 0, 0)),
                  pl.BlockSpec((1,) + v.shape[1:], lambda b, s: (b, 0, 0)),
                  _const_spec(wo.shape), _const_spec((1, D))],
        out_specs=pl.BlockSpec((1, tm, D), lambda b, s: (b, s, 0)),
        compiler_params=_cparams(("arbitrary", "arbitrary")),
        name="xattn",
    )(x, pre, wq, kT, v, wo, post)


def _mixer_weights(w_in, pool_w, dsa_w_uk, dsa_w_uv):
    D = w_in.shape[0]
    sizes = (BRANCH_WIDTH, DSA_HEADS * HEAD_DIM, DSA_LATENT, IDX_HEADS * IDX_DIM, IDX_DIM, IDX_HEADS,
             SB_HEADS * HEAD_DIM, SB_HEADS * HEAD_DIM, SB_HEADS * HEAD_DIM,
             SWA_HEADS * HEAD_DIM, SWA_KV_HEADS * HEAD_DIM, SWA_KV_HEADS * HEAD_DIM)
    cols, acc = [], 0
    for s in sizes:
        cols.append(w_in[:, acc:acc + s])
        acc += s
    (w_pool, w_dq, w_ckv, w_iq, w_ik, w_iw, w_sbq, w_sbk, w_sbv, w_swq, w_swk, w_swv) = cols
    w_gate = w_in[:, acc:]
    group = SWA_HEADS // SWA_KV_HEADS

    def expand_kv(w):
        return jnp.repeat(w.reshape(D, SWA_KV_HEADS, HEAD_DIM), group, axis=1).reshape(D, SWA_HEADS * HEAD_DIM)

    ikw_pad = jnp.zeros((D, 128 - IDX_DIM - IDX_HEADS), w_in.dtype)
    w_mix = jnp.concatenate([w_pool, w_dq, w_ckv, w_iq, w_ik, w_iw, ikw_pad, w_sbq, w_sbk, w_sbv,
                             w_swq, expand_kv(w_swk), expand_kv(w_swv)], axis=1).astype(BF16)
    gd = pool_w.shape[1]
    pool_bd = jnp.zeros((BRANCH_WIDTH, BRANCH_WIDTH), F32)
    for g in range(pool_w.shape[0]):
        pool_bd = pool_bd.at[g * gd:(g + 1) * gd, g * gd:(g + 1) * gd].set(pool_w[g])
    wuk_bd = jnp.zeros((DSA_HEADS * HEAD_DIM, DSA_HEADS * DSA_LATENT), F32)
    wuv_pad = jnp.zeros((DSA_HEADS, DSA_LATENT, DSA_HEADS * HEAD_DIM), F32)
    for h in range(DSA_HEADS):
        wuk_bd = wuk_bd.at[h * HEAD_DIM:(h + 1) * HEAD_DIM, h * DSA_LATENT:(h + 1) * DSA_LATENT].set(dsa_w_uk[:, h, :].T)
        wuv_pad = wuv_pad.at[h, :, h * HEAD_DIM:(h + 1) * HEAD_DIM].set(dsa_w_uv[:, h, :])
    return w_mix, w_gate.astype(BF16), pool_bd.astype(BF16), wuk_bd.astype(BF16), wuv_pad.astype(BF16)


def kernel(x, mem, ffn1_pre, ffn1_w_up, ffn1_w_down, ffn1_post, mix_pre, w_in, pool_w, pool_scale, dsa_kv_norm, dsa_w_uk, dsa_w_uv, swa_sinks, w_branch, w_out, mix_post, xattn_pre, mem_norm, xattn_w_q, xattn_w_kv, xattn_w_o, xattn_post, ffn2_pre, ffn2_w_up, ffn2_w_down, ffn2_post):
    B, S, D = x.shape
    depth = w_in.shape[0]
    T = B * S
    tm_ffn = min(1024, T)
    tf = 256
    tm_proj = min(512, S)
    tm_merge = min(512, T)
    tm_x = min(512, S)
    row = lambda a: a.reshape(1, -1)
    for l in range(depth):
        x2 = _ffn(x.reshape(T, D), row(ffn1_pre[l]), ffn1_w_up[l].astype(BF16), ffn1_w_down[l].astype(BF16),
                  row(ffn1_post[l]), tm=tm_ffn, tf=tf)
        w_mix, w_gate, pool_bd, wuk_bd, wuv_pad = _mixer_weights(w_in[l], pool_w[l], dsa_w_uk[l], dsa_w_uv[l])
        (oa, qlat, c, cT, iq, iw, ikT, sbq, sbkT, sbv, swq, swkT, swv) = _proj(
            x2.reshape(B, S, D), row(mix_pre[l]), w_mix, pool_bd, row(pool_scale[l]), row(dsa_kv_norm[l]),
            wuk_bd, tm=tm_proj)
        ob = _dsa(qlat, iq, iw, ikT, cT, c, wuv_pad)
        oc = _sb(sbq, sbkT, sbv)
        od = _swa(swa_sinks[l], swq, swkT, swv)
        W = BRANCH_WIDTH
        x2 = _merge(x2, row(mix_pre[l]), w_gate, oa.reshape(T, W), ob.reshape(T, W), oc.reshape(T, W),
                    od.reshape(T, W), w_branch[l].astype(BF16), w_out[l].astype(BF16), row(mix_post[l]),
                    tm=tm_merge)
        mkT, mv = _memkv(mem, row(mem_norm[l]), xattn_w_kv[l].astype(BF16))
        x3 = _xattn(x2.reshape(B, S, D), row(xattn_pre[l]), xattn_w_q[l].astype(BF16), mkT, mv,
                    xattn_w_o[l].astype(BF16), row(xattn_post[l]), tm=tm_x)
        x = _ffn(x3.reshape(T, D), row(ffn2_pre[l]), ffn2_w_up[l].astype(BF16), ffn2_w_down[l].astype(BF16),
                 row(ffn2_post[l]), tm=tm_ffn, tf=tf).reshape(B, S, D)
    return x
```

```python
import functools

import jax
import jax.numpy as jnp
from jax import lax
from jax.experimental import pallas as pl
from jax.experimental.pallas import tpu as pltpu

F32 = jnp.float32
BF16 = jnp.bfloat16
I32 = jnp.int32
I16 = jnp.int16

RMS_EPS = 1e-6
HEAD_DIM = 64
BRANCH_WIDTH = 256
N_BRANCH = 4
POOL_WINDOWS = (2, 4, 8, 16)
POOL_HALO = 16
DSA_HEADS = 4
DSA_LATENT = 128
IDX_HEADS = 8
IDX_DIM = 32
DSA_TOPK_MAX = 256
SB_HEADS = 4
SWA_HEADS = 4
SWA_KV_HEADS = 2
SWA_WINDOW = 128
MEM_HEADS = 4

BLK = 128
GRP = 4
GK = GRP * BLK
VMEM_LIMIT = 48 * 1024 * 1024
MASKED = -1e30
INT_MIN = -(2 ** 31)
ONES_ROWS = 16
SCORE_MASKED = -3.3895313892515355e38
KEY_MASKED = (0xFF7F0000 - (1 << 32)) ^ 0x7FFFFFFF
LOG2E = 1.4426950408889634
EXP_ZERO_BELOW = -104.0


def _cparams(sem):
    return pltpu.CompilerParams(dimension_semantics=sem, vmem_limit_bytes=VMEM_LIMIT)


def _rms(x, g):
    return x * lax.rsqrt(jnp.mean(x * x, axis=-1, keepdims=True) + RMS_EPS) * g


def _const_spec(shape):
    nd = len(shape)
    return pl.BlockSpec(shape, lambda *_: (0,) * nd)


def _head_stack(q, heads):
    rows, width = q.shape
    per = width // heads
    lane = lax.broadcasted_iota(I32, (rows, width), 1)
    zero = jnp.zeros_like(q)
    return jnp.concatenate(
        [jnp.where((lane >= h * per) & (lane < (h + 1) * per), q, zero) for h in range(heads)], axis=0)


def _head_merge(o, heads):
    rows = o.shape[0] // heads
    width = o.shape[1]
    per = width // heads
    lane = lax.broadcasted_iota(I32, (rows, width), 1)
    out = jnp.zeros((rows, width), o.dtype)
    for h in range(heads):
        out = jnp.where((lane >= h * per) & (lane < (h + 1) * per), o[h * rows:(h + 1) * rows], out)
    return out


def _ffn_kernel(x_ref, pre_ref, wg_ref, wu_ref, wd_ref, post_ref, o_ref, hn_ref, acc_ref):
    j = pl.program_id(1)

    @pl.when(j == 0)
    def _():
        hn_ref[...] = _rms(x_ref[...], pre_ref[...]).astype(BF16)
        acc_ref[...] = jnp.zeros_like(acc_ref)

    hn = hn_ref[...]
    g = jnp.dot(hn, wg_ref[...], preferred_element_type=F32)
    u = jnp.dot(hn, wu_ref[...], preferred_element_type=F32)
    a = (g * jax.nn.sigmoid(g) * u).astype(BF16)
    acc_ref[...] += jnp.dot(a, wd_ref[...], preferred_element_type=F32)

    @pl.when(j == pl.num_programs(1) - 1)
    def _():
        o_ref[...] = x_ref[...] + 0.5 * _rms(acc_ref[...], post_ref[...])


def _ffn(x2, pre, w_up, w_down, post, *, tm, tf):
    T, D = x2.shape
    F = w_down.shape[0]
    nf = F // tf
    return pl.pallas_call(
        _ffn_kernel,
        out_shape=jax.ShapeDtypeStruct((T, D), F32),
        grid=(T // tm, nf),
        in_specs=[
            pl.BlockSpec((tm, D), lambda i, j: (i, 0)),
            pl.BlockSpec((1, D), lambda i, j: (0, 0)),
            pl.BlockSpec((D, tf), lambda i, j: (0, j)),
            pl.BlockSpec((D, tf), lambda i, j: (0, j + nf)),
            pl.BlockSpec((tf, D), lambda i, j: (j, 0)),
            pl.BlockSpec((1, D), lambda i, j: (0, 0)),
        ],
        out_specs=pl.BlockSpec((tm, D), lambda i, j: (i, 0)),
        scratch_shapes=[pltpu.VMEM((tm, D), BF16), pltpu.VMEM((tm, D), F32)],
        compiler_params=_cparams(("arbitrary", "arbitrary")),
        name="ffn",
    )(x2, pre, w_up, w_up, w_down, post)


_SEG = dict(pool=(0, 256), dq=(256, 512), ckv=(512, 640), iq=(640, 896), ikw=(896, 1024),
            sbq=(1024, 1280), sbk=(1280, 1536), sbv=(1536, 1792),
            swq=(1792, 2048), swk=(2048, 2304), swv=(2304, 2560))
_PROJ_W = 2560


def _proj_kernel(x_ref, pre_ref, w_ref, poolw_ref, pools_ref, kvn_ref, wuk_ref,
                 oa_ref, qlatT_ref, c_ref, cT_ref, iqT_ref, ikw_ref, iwT_ref,
                 sbq_ref, sbkT_ref, sbv_ref, swq_ref, swkT_ref, swv_ref, halo_ref):
    si = pl.program_id(1)
    tm = x_ref.shape[1]
    nb = tm // BLK
    hn = _rms(x_ref[0], pre_ref[...]).astype(BF16)
    p = jnp.dot(hn, w_ref[...], preferred_element_type=F32)

    def seg(name):
        a, b = _SEG[name]
        return p[:, a:b]

    @pl.when(si == 0)
    def _():
        halo_ref[...] = jnp.zeros_like(halo_ref)

    u = seg("pool")
    s1 = jnp.concatenate([halo_ref[...], u], axis=0)
    halo_ref[...] = u[tm - POOL_HALO:, :]
    s2 = s1[1:] + s1[:-1]
    s4 = s2[2:] + s2[:-2]
    s8 = s4[4:] + s4[:-4]
    s16 = s8[8:] + s8[:-8]
    lane = lax.broadcasted_iota(I32, (tm, BRANCH_WIDTH), 1)
    t1 = (lax.broadcasted_iota(I32, (tm, BRANCH_WIDTH), 0) + si * tm + 1).astype(F32)
    gw = BRANCH_WIDTH // len(POOL_WINDOWS)
    wsum = jnp.where(lane < gw, s2[-tm:], jnp.where(lane < 2 * gw, s4[-tm:],
                     jnp.where(lane < 3 * gw, s8[-tm:], s16[-tm:])))
    win = jnp.where(lane < gw, float(POOL_WINDOWS[0]), jnp.where(lane < 2 * gw, float(POOL_WINDOWS[1]),
                    jnp.where(lane < 3 * gw, float(POOL_WINDOWS[2]), float(POOL_WINDOWS[3]))))
    pooled = wsum / jnp.minimum(t1, win) - u
    mixed = jnp.dot(pooled.astype(BF16), poolw_ref[...], preferred_element_type=F32)
    oa_ref[0] = (mixed * pools_ref[...]).astype(BF16)

    qlat = jnp.dot(seg("dq").astype(BF16), wuk_ref[...], preferred_element_type=F32) * (HEAD_DIM ** -0.5 * LOG2E)
    c = _rms(seg("ckv"), kvn_ref[...])
    c_ref[0] = c.astype(BF16)
    iq = seg("iq") * IDX_DIM ** -0.5
    ikw = seg("ikw")
    ikw_ref[0] = ikw.astype(BF16)
    sbq_ref[0] = (seg("sbq") * HEAD_DIM ** -0.5).astype(BF16)
    sbv_ref[0] = seg("sbv").astype(BF16)
    swq_ref[0] = (seg("swq") * HEAD_DIM ** -0.5).astype(BF16)
    swv_ref[0] = seg("swv").astype(BF16)
    sbk = seg("sbk")
    swk = seg("swk")
    for g in range(tm // GK):
        cT_ref[0, g, 0:DSA_LATENT, :] = c[g * GK:(g + 1) * GK].T.astype(BF16)
        cT_ref[0, g, DSA_LATENT:, :] = jnp.ones((ONES_ROWS, GK), BF16)
    for b in range(nb):
        rows = slice(b * BLK, (b + 1) * BLK)
        for h in range(DSA_HEADS):
            lanes = slice(h * DSA_LATENT, (h + 1) * DSA_LATENT)
            qlatT_ref[0, b, :, lanes] = qlat[rows, lanes].T.astype(BF16)
        iqT_ref[0, b] = iq[rows].T.astype(BF16)
        iwT_ref[0, b] = ikw[rows].T[IDX_DIM:IDX_DIM + IDX_HEADS] * IDX_HEADS ** -0.5
        sbkT_ref[0, b] = sbk[rows].T.astype(BF16)
        swkT_ref[0, b] = swk[rows].T.astype(BF16)


def _proj(x, pre, w, poolw, pools, kvn, wuk, *, tm):
    B, S, D = x.shape
    nq = S // BLK
    nb = tm // BLK

    def rows(width, dtype):
        return jax.ShapeDtypeStruct((B, S, width), dtype), pl.BlockSpec((1, tm, width), lambda b, s: (b, s, 0))

    def blocks(height, width=BLK, dtype=BF16):
        return (jax.ShapeDtypeStruct((B, nq, height, width), dtype),
                pl.BlockSpec((1, nb, height, width), lambda b, s: (b, s, 0, 0)))

    outs = [rows(256, BF16),
            blocks(DSA_LATENT, DSA_HEADS * BLK),
            rows(128, BF16),
            (jax.ShapeDtypeStruct((B, S // GK, DSA_LATENT + ONES_ROWS, GK), BF16),
             pl.BlockSpec((1, tm // GK, DSA_LATENT + ONES_ROWS, GK), lambda b, s: (b, s, 0, 0))),
            blocks(256),
            rows(128, BF16),
            blocks(IDX_HEADS, BLK, F32),
            rows(256, BF16),
            blocks(256),
            rows(256, BF16),
            rows(256, BF16),
            blocks(256),
            rows(256, BF16)]
    return pl.pallas_call(
        _proj_kernel,
        out_shape=[o[0] for o in outs],
        grid=(B, S // tm),
        in_specs=[pl.BlockSpec((1, tm, D), lambda b, s: (b, s, 0)),
                  _const_spec((1, D)), _const_spec(w.shape), _const_spec(poolw.shape),
                  _const_spec(pools.shape), _const_spec(kvn.shape), _const_spec(wuk.shape)],
        out_specs=[o[1] for o in outs],
        scratch_shapes=[pltpu.VMEM((POOL_HALO, BRANCH_WIDTH), F32)],
        compiler_params=_cparams(("arbitrary", "arbitrary")),
        name="proj",
    )(x, pre, w, poolw, pools, kvn, wuk)


def _sort_key(s):
    b = pltpu.bitcast(s, I32)
    return b ^ ((b >> 31) & 0x7FFFFFFF)


def _dsa_kernel(qlatT_ref, iqT_ref, iwT_ref, ikw_ref, cT_ref, c_ref, wuv_ref, o_ref,
                key_ref, hi_ref, s_ref, acc_ref, *, topk):
    n = pl.program_id(1)
    g_last = n // GRP
    ngrp = g_last + 1
    kf = float(topk)
    krow = lax.broadcasted_iota(I32, (BLK, BLK), 0)
    qcol = lax.broadcasted_iota(I32, (BLK, BLK), 1)

    def rows(g):
        return pl.ds(pl.multiple_of(g * GK, GK), GK)

    iqT = iqT_ref[0, 0]
    zpad = jnp.zeros((BLK - IDX_DIM, BLK), BF16)
    iqw = jnp.concatenate([jnp.concatenate([iqT[h * IDX_DIM:(h + 1) * IDX_DIM], zpad], axis=0)
                           for h in range(IDX_HEADS)], axis=1)
    iwT = iwT_ref[0, 0]
    iwb = [jnp.broadcast_to(iwT[h:h + 1, :], (GK, BLK)) for h in range(IDX_HEADS)]

    def score_group(g):
        kj = ikw_ref[0, rows(g), :]
        acc = None
        for hp in range(IDX_HEADS // 2):
            lg = jnp.dot(kj, iqw[:, hp * 2 * BLK:(hp + 1) * 2 * BLK], preferred_element_type=F32)
            for hh in range(2):
                term = iwb[2 * hp + hh] * jnp.maximum(lg[:, hh * BLK:(hh + 1) * BLK], 0.0)
                acc = term if acc is None else acc + term
        return jnp.where(acc == 0.0, 0.0, acc)

    def store_keys(g, s):
        key = _sort_key(s)
        key_ref[rows(g), :] = key
        hi_ref[rows(g), :] = (key >> 16).astype(I16)

    def score_body(g, carry):
        store_keys(g, score_group(g))
        return carry

    lax.fori_loop(0, g_last, score_body, 0)
    kpos = lax.broadcasted_iota(I32, (GK, BLK), 0) + g_last * GK
    qpos = lax.broadcasted_iota(I32, (GK, BLK), 1) + n * BLK
    store_keys(g_last, jnp.where(kpos <= qpos, score_group(g_last), SCORE_MASKED))

    def count_hi(cand16):
        cb = jnp.broadcast_to(cand16.astype(I16), (BLK, BLK))
        one = jnp.ones((BLK, BLK), I16)
        zero = jnp.zeros((BLK, BLK), I16)

        def body(g, a):
            r0 = pl.multiple_of(g * GK, GK)
            for i in range(GRP):
                a = a + jnp.where(hi_ref[pl.ds(r0 + i * BLK, BLK), :] >= cb, one, zero)
            return a

        a = lax.fori_loop(0, ngrp, body, zero)
        return jnp.sum(a.astype(F32), axis=0, keepdims=True)

    total = jnp.full((1, BLK), 1.0, F32) * (ngrp * GK).astype(F32)
    cnt0 = count_hi(jnp.zeros((1, BLK), I32))
    ok0 = cnt0 >= kf
    t16 = jnp.where(ok0, 0, -(2 ** 15)).astype(I32)
    cnt_thr = jnp.where(ok0, cnt0, total)

    def bisect_hi(i, carry):
        t, ct = carry
        cand = t + jnp.left_shift(jnp.int32(1), 14 - i)
        cnt = count_hi(cand)
        ok = cnt >= kf
        return jnp.where(ok, cand, t), jnp.where(ok, cnt, ct)

    t16, cnt_thr = lax.fori_loop(0, 15, bisect_hi, (t16, cnt_thr))

    def count_ge(cand):
        cb = jnp.broadcast_to(cand, (BLK, BLK))

        def body(g, a):
            r0 = pl.multiple_of(g * GK, GK)
            for i in range(GRP):
                a = a + jnp.where(key_ref[pl.ds(r0 + i * BLK, BLK), :] >= cb, 1.0, 0.0)
            return a

        a = lax.fori_loop(0, ngrp, body, jnp.zeros((BLK, BLK), F32))
        return jnp.sum(a, axis=0, keepdims=True)

    def bisect_lo(i, carry):
        t, ct = carry
        cand = t + jnp.left_shift(jnp.int32(1), 15 - i)
        cnt = count_ge(cand)
        ok = cnt >= kf
        return jnp.where(ok, cand, t), jnp.where(ok, cnt, ct)

    thr, cnt_thr = lax.fori_loop(0, 16, bisect_lo, (jnp.left_shift(t16, 16), cnt_thr))
    cnt_gt = count_ge(thr + 1)
    need = kf - cnt_gt
    n_eq = cnt_thr - cnt_gt

    @pl.when(jnp.max(jnp.where(thr > KEY_MASKED, n_eq - need, 0.0)) > 0.0)
    def _():
        tb = jnp.broadcast_to(thr, (BLK, BLK))
        incl = jnp.where(qcol <= krow, 1.0, 0.0).astype(BF16)

        def body(j, run):
            r = pl.ds(pl.multiple_of(j * BLK, BLK), BLK)
            k = key_ref[r, :]
            eq = k == tb
            eqf = jnp.where(eq, 1.0, 0.0)
            pref = jnp.dot(incl, eqf.astype(BF16), preferred_element_type=F32)
            keep = (run + pref) <= need
            key_ref[r, :] = jnp.where(eq & jnp.logical_not(keep), INT_MIN, k)
            return run + jnp.sum(eqf, axis=0, keepdims=True)

        lax.fori_loop(0, n + 1, body, jnp.zeros((1, BLK), F32))

    tsel = jnp.broadcast_to(jnp.maximum(thr, KEY_MASKED + 1), (GK, BLK))

    qsT = qlatT_ref[0, 0]
    W = DSA_HEADS * BLK
    acc_ref[...] = jnp.zeros_like(acc_ref)

    def logits(g, m_run):
        s = jnp.dot(c_ref[0, rows(g), :], qsT, preferred_element_type=F32)
        sel = key_ref[rows(g), :] >= tsel
        s = jnp.where(jnp.concatenate([sel] * DSA_HEADS, axis=1), s, MASKED)
        s_ref[g % 2] = s
        return jnp.maximum(m_run, jnp.max(s, axis=0, keepdims=True))

    def accumulate(g, m_old, m_new):
        alpha = jnp.exp2(m_old - m_new)
        p = jnp.exp2(s_ref[g % 2] - m_new).astype(BF16)
        acc_ref[...] = alpha * acc_ref[...] + jnp.dot(cT_ref[0, g], p, preferred_element_type=F32)

    m_none = jnp.full((1, W), MASKED, F32)

    def attn_body(g, carry):
        m_old, m_new = carry
        accumulate(g, m_old, m_new)
        return m_new, logits(g + 1, m_new)

    m_old, m_new = lax.fori_loop(0, g_last, attn_body, (m_none, logits(0, m_none)))
    accumulate(g_last, m_old, m_new)
    l = acc_ref[DSA_LATENT:DSA_LATENT + 1, :]
    olatT = acc_ref[0:DSA_LATENT, :] * (1.0 / l)
    out = None
    for h in range(DSA_HEADS):
        olat = olatT[:, h * BLK:(h + 1) * BLK].T.astype(BF16)
        y = jnp.dot(olat, wuv_ref[h], preferred_element_type=F32)
        out = y if out is None else out + y
    o_ref[0] = out.astype(BF16)


def _dsa(qlatT, iqT, iwT, ikw, cT, c, wuv):
    B, S, _ = c.shape
    assert S % GK == 0
    topk = min(DSA_TOPK_MAX, S // 4)
    blk4 = lambda a: pl.BlockSpec((1, 1) + a.shape[2:], lambda b, n: (b, n, 0, 0))
    return pl.pallas_call(
        functools.partial(_dsa_kernel, topk=topk),
        out_shape=jax.ShapeDtypeStruct((B, S, BRANCH_WIDTH), BF16),
        grid=(B, S // BLK),
        in_specs=[blk4(qlatT), blk4(iqT), blk4(iwT),
                  pl.BlockSpec((1,) + ikw.shape[1:], lambda b, n: (b, 0, 0)),
                  pl.BlockSpec((1,) + cT.shape[1:], lambda b, n: (b, 0, 0, 0)),
                  pl.BlockSpec((1,) + c.shape[1:], lambda b, n: (b, 0, 0)),
                  _const_spec(wuv.shape)],
        out_specs=pl.BlockSpec((1, BLK, BRANCH_WIDTH), lambda b, n: (b, n, 0)),
        scratch_shapes=[pltpu.VMEM((S, BLK), I32), pltpu.VMEM((S, BLK), I16),
                        pltpu.VMEM((2, GK, DSA_HEADS * BLK), F32),
                        pltpu.VMEM((DSA_LATENT + ONES_ROWS, DSA_HEADS * BLK), F32)],
        compiler_params=_cparams(("arbitrary", "arbitrary")),
        name="dsa",
    )(qlatT, iqT, iwT, ikw, cT, c, wuv)


def _sb_kernel(q_ref, kT_ref, v_ref, o_ref, acc_ref):
    n = pl.program_id(1)
    H = SB_HEADS
    qm = _head_stack(q_ref[0], H)
    row = lax.broadcasted_iota(I32, (BLK, BLK), 0)
    col = lax.broadcasted_iota(I32, (BLK, BLK), 1)
    after = jnp.where(row > col, 1.0, 0.0).astype(BF16)
    strict = jnp.concatenate([col < row] * H, axis=0)

    def step(j, run, diag):
        z = jnp.dot(qm, kT_ref[0, j], preferred_element_type=F32)
        t = jnp.log1p(jnp.exp(-jnp.abs(z)))
        l1m = -(jnp.maximum(z, 0.0) + t)
        lsg = jnp.minimum(z, 0.0) - t
        if diag:
            l1m = jnp.where(strict, l1m, 0.0)
        hi = l1m.astype(BF16)
        r1 = l1m - hi.astype(F32)
        mid = r1.astype(BF16)
        lo = (r1 - mid.astype(F32)).astype(BF16)
        suf = (jnp.dot(hi, after, preferred_element_type=F32)
               + jnp.dot(mid, after, preferred_element_type=F32)
               + jnp.dot(lo, after, preferred_element_type=F32))
        a = jnp.exp(lsg + run + suf)
        if diag:
            a = jnp.where(strict, a, 0.0)
        vj = v_ref[0, pl.ds(pl.multiple_of(j * BLK, BLK), BLK), :]
        pv = jnp.dot(a.astype(BF16), vj, preferred_element_type=F32)
        if diag:
            acc_ref[...] = pv
        else:
            acc_ref[...] += pv
        return run + jnp.sum(l1m, axis=1, keepdims=True)

    run0 = step(n, jnp.zeros((H * BLK, 1), F32), True)

    def cond(carry):
        j, _, alive = carry
        return jnp.logical_and(j >= 0, alive)

    def body(carry):
        j, run, _ = carry
        run = step(j, run, False)
        return j - 1, run, jnp.max(run) > EXP_ZERO_BELOW

    lax.while_loop(cond, body, (n - 1, run0, jnp.max(run0) > EXP_ZERO_BELOW))
    o_ref[0] = _head_merge(acc_ref[...], H).astype(BF16)


def _sb(q, kT, v):
    B, S, W = q.shape
    nq = S // BLK
    return pl.pallas_call(
        _sb_kernel,
        out_shape=jax.ShapeDtypeStruct((B, S, W), BF16),
        grid=(B, nq),
        in_specs=[pl.BlockSpec((1, BLK, W), lambda b, n: (b, n, 0)),
                  pl.BlockSpec((1,) + kT.shape[1:], lambda b, n: (b, 0, 0, 0)),
                  pl.BlockSpec((1,) + v.shape[1:], lambda b, n: (b, 0, 0))],
        out_specs=pl.BlockSpec((1, BLK, W), lambda b, n: (b, n, 0)),
        scratch_shapes=[pltpu.VMEM((SB_HEADS * BLK, W), F32)],
        compiler_params=_cparams(("arbitrary", "arbitrary")),
        name="sb",
    )(q, kT, v)


def _swa_kernel(sink_ref, q_ref, kTp_ref, kTc_ref, vp_ref, vc_ref, o_ref):
    n = pl.program_id(1)
    H = SWA_HEADS
    qm = _head_stack(q_ref[0], H)
    kT = jnp.concatenate([kTp_ref[0, 0], kTc_ref[0, 0]], axis=1)
    s = jnp.dot(qm, kT, preferred_element_type=F32)
    r = lax.broadcasted_iota(I32, (BLK, 2 * BLK), 0)
    c = lax.broadcasted_iota(I32, (BLK, 2 * BLK), 1)
    rel = r + BLK - c
    ok = (rel >= 0) & (rel < SWA_WINDOW) & ((c >= BLK) | (n > 0))
    s = jnp.where(jnp.concatenate([ok] * H, axis=0), s, MASKED)
    sink = jnp.concatenate([jnp.full((BLK, 1), sink_ref[h], F32) for h in range(H)], axis=0)
    m = jnp.maximum(jnp.max(s, axis=1, keepdims=True), sink)
    p = jnp.exp(s - m)
    denom = jnp.sum(p, axis=1, keepdims=True) + jnp.exp(sink - m)
    p = (p / denom).astype(BF16)
    v = jnp.concatenate([vp_ref[0], vc_ref[0]], axis=0)
    o_ref[0] = _head_merge(jnp.dot(p, v, preferred_element_type=F32), H).astype(BF16)


def _swa(sinks, q, kT, v):
    B, S, W = q.shape
    nq = S // BLK
    prev = lambda b, n: (b, jnp.maximum(n - 1, 0), 0, 0)
    prev3 = lambda b, n: (b, jnp.maximum(n - 1, 0), 0)
    return pl.pallas_call(
        _swa_kernel,
        out_shape=jax.ShapeDtypeStruct((B, S, W), BF16),
        grid=(B, nq),
        in_specs=[pl.BlockSpec(memory_space=pltpu.SMEM),
                  pl.BlockSpec((1, BLK, W), lambda b, n: (b, n, 0)),
                  pl.BlockSpec((1, 1, W, BLK), prev),
                  pl.BlockSpec((1, 1, W, BLK), lambda b, n: (b, n, 0, 0)),
                  pl.BlockSpec((1, BLK, W), prev3),
                  pl.BlockSpec((1, BLK, W), lambda b, n: (b, n, 0))],
        out_specs=pl.BlockSpec((1, BLK, W), lambda b, n: (b, n, 0)),
        compiler_params=_cparams(("arbitrary", "arbitrary")),
        name="swa",
    )(sinks, q, kT, kT, v, v)


def _merge_kernel(x_ref, pre_ref, wg_ref, oa_ref, ob_ref, oc_ref, od_ref, wb_ref, wo_ref, post_ref, o_ref):
    x = x_ref[...]
    D = x.shape[1]
    hn = _rms(x, pre_ref[...]).astype(BF16)
    merged = None
    for i, br in enumerate((oa_ref, ob_ref, oc_ref, od_ref)):
        gate = jax.nn.sigmoid(jnp.dot(hn, wg_ref[:, i * D:(i + 1) * D], preferred_element_type=F32))
        y = jnp.dot(br[...], wb_ref[i], preferred_element_type=F32)
        merged = gate * y if merged is None else merged + gate * y
    m = jnp.dot(merged.astype(BF16), wo_ref[...], preferred_element_type=F32)
    o_ref[...] = x + _rms(m, post_ref[...])


def _merge(x2, pre, wg, oa, ob, oc, od, wb, wo, post, *, tm):
    T, D = x2.shape
    W = oa.shape[1]
    br = pl.BlockSpec((tm, W), lambda i: (i, 0))
    return pl.pallas_call(
        _merge_kernel,
        out_shape=jax.ShapeDtypeStruct((T, D), F32),
        grid=(T // tm,),
        in_specs=[pl.BlockSpec((tm, D), lambda i: (i, 0)), _const_spec((1, D)), _const_spec(wg.shape),
                  br, br, br, br, _const_spec(wb.shape), _const_spec(wo.shape), _const_spec((1, D))],
        out_specs=pl.BlockSpec((tm, D), lambda i: (i, 0)),
        compiler_params=_cparams(("arbitrary",)),
        name="merge",
    )(x2, pre, wg, oa, ob, oc, od, wb, wo, post)


def _memkv_kernel(mem_ref, g_ref, wkv_ref, kT_ref, v_ref):
    mn = _rms(mem_ref[0], g_ref[...]).astype(BF16)
    kv = jnp.dot(mn, wkv_ref[...], preferred_element_type=F32)
    W = kv.shape[1] // 2
    kT_ref[0] = kv[:, :W].T.astype(BF16)
    v_ref[0] = kv[:, W:].astype(BF16)


def _memkv(mem, g, wkv):
    B, M, D = mem.shape
    W = wkv.shape[1] // 2
    return pl.pallas_call(
        _memkv_kernel,
        out_shape=[jax.ShapeDtypeStruct((B, W, M), BF16), jax.ShapeDtypeStruct((B, M, W), BF16)],
        grid=(B,),
        in_specs=[pl.BlockSpec((1, M, D), lambda b: (b, 0, 0)), _const_spec((1, D)), _const_spec(wkv.shape)],
        out_specs=[pl.BlockSpec((1, W, M), lambda b: (b, 0, 0)), pl.BlockSpec((1, M, W), lambda b: (b, 0, 0))],
        compiler_params=_cparams(("arbitrary",)),
        name="memkv",
    )(mem, g, wkv)


def _xattn_kernel(x_ref, pre_ref, wq_ref, kT_ref, v_ref, wo_ref, post_ref, o_ref):
    x = x_ref[0]
    hn = _rms(x, pre_ref[...]).astype(BF16)
    q = (jnp.dot(hn, wq_ref[...], preferred_element_type=F32) * HEAD_DIM ** -0.5).astype(BF16)
    s = jnp.dot(_head_stack(q, MEM_HEADS), kT_ref[0], preferred_element_type=F32)
    p = jnp.exp(s - jnp.max(s, axis=1, keepdims=True))
    p = (p / jnp.sum(p, axis=1, keepdims=True)).astype(BF16)
    o = _head_merge(jnp.dot(p, v_ref[0], preferred_element_type=F32), MEM_HEADS)
    c = jnp.dot(o.astype(BF16), wo_ref[...], preferred_element_type=F32)
    o_ref[0] = x + _rms(c, post_ref[...])


def _xattn(x, pre, wq, kT, v, wo, post, *, tm):
    B, S, D = x.shape
    return pl.pallas_call(
        _xattn_kernel,
        out_shape=jax.ShapeDtypeStruct((B, S, D), F32),
        grid=(B, S // tm),
        in_specs=[pl.BlockSpec((1, tm, D), lambda b, s: (b, s, 0)), _const_spec((1, D)), _const_spec(wq.shape),
                  pl.BlockSpec((1,) + kT.shape[1:], lambda b, s: (b, 0, 0)),
                  pl.BlockSpec((1,) + v.shape[1:], lambda b, s: (b, 0, 0)),
                  _const_spec(wo.shape), _const_spec((1, D))],
        out_specs=pl.BlockSpec((1, tm, D), lambda b, s: (b, s, 0)),
        compiler_params=_cparams(("arbitrary", "arbitrary")),
        name="xattn",
    )(x, pre, wq, kT, v, wo, post)


def _mixer_weights(w_in, pool_w, dsa_w_uk, dsa_w_uv):
    D = w_in.shape[0]
    sizes = (BRANCH_WIDTH, DSA_HEADS * HEAD_DIM, DSA_LATENT, IDX_HEADS * IDX_DIM, IDX_DIM, IDX_HEADS,
             SB_HEADS * HEAD_DIM, SB_HEADS * HEAD_DIM, SB_HEADS * HEAD_DIM,
             SWA_HEADS * HEAD_DIM, SWA_KV_HEADS * HEAD_DIM, SWA_KV_HEADS * HEAD_DIM)
    cols, acc = [], 0
    for s in sizes:
        cols.append(w_in[:, acc:acc + s])
        acc += s
    (w_pool, w_dq, w_ckv, w_iq, w_ik, w_iw, w_sbq, w_sbk, w_sbv, w_swq, w_swk, w_swv) = cols
    w_gate = w_in[:, acc:]
    group = SWA_HEADS // SWA_KV_HEADS

    def expand_kv(w):
        return jnp.repeat(w.reshape(D, SWA_KV_HEADS, HEAD_DIM), group, axis=1).reshape(D, SWA_HEADS * HEAD_DIM)

    ikw_pad = jnp.zeros((D, 128 - IDX_DIM - IDX_HEADS), w_in.dtype)
    w_mix = jnp.concatenate([w_pool, w_dq, w_ckv, w_iq, w_ik, w_iw, ikw_pad, w_sbq, w_sbk, w_sbv,
                             w_swq, expand_kv(w_swk), expand_kv(w_swv)], axis=1).astype(BF16)
    gd = pool_w.shape[1]
    pool_bd = jnp.zeros((BRANCH_WIDTH, BRANCH_WIDTH), F32)
    for g in range(pool_w.shape[0]):
        pool_bd = pool_bd.at[g * gd:(g + 1) * gd, g * gd:(g + 1) * gd].set(pool_w[g])
    wuk_bd = jnp.zeros((DSA_HEADS * HEAD_DIM, DSA_HEADS * DSA_LATENT), F32)
    wuv_pad = jnp.zeros((DSA_HEADS, DSA_LATENT, DSA_HEADS * HEAD_DIM), F32)
    for h in range(DSA_HEADS):
        wuk_bd = wuk_bd.at[h * HEAD_DIM:(h + 1) * HEAD_DIM, h * DSA_LATENT:(h + 1) * DSA_LATENT].set(dsa_w_uk[:, h, :].T)
        wuv_pad = wuv_pad.at[h, :, h * HEAD_DIM:(h + 1) * HEAD_DIM].set(dsa_w_uv[:, h, :])
    return w_mix, w_gate.astype(BF16), pool_bd.astype(BF16), wuk_bd.astype(BF16), wuv_pad.astype(BF16)


def kernel(x, mem, ffn1_pre, ffn1_w_up, ffn1_w_down, ffn1_post, mix_pre, w_in, pool_w, pool_scale, dsa_kv_norm, dsa_w_uk, dsa_w_uv, swa_sinks, w_branch, w_out, mix_post, xattn_pre, mem_norm, xattn_w_q, xattn_w_kv, xattn_w_o, xattn_post, ffn2_pre, ffn2_w_up, ffn2_w_down, ffn2_post):
    B, S, D = x.shape
    depth = w_in.shape[0]
    T = B * S
    tm_ffn = min(1024, T)
    tf = 256
    tm_proj = min(512, S)
    tm_merge = min(512, T)
    tm_x = min(512, S)
    row = lambda a: a.reshape(1, -1)
    for l in range(depth):
        x2 = _ffn(x.reshape(T, D), row(ffn1_pre[l]), ffn1_w_up[l].astype(BF16), ffn1_w_down[l].astype(BF16),
                  row(ffn1_post[l]), tm=tm_ffn, tf=tf)
        w_mix, w_gate, pool_bd, wuk_bd, wuv_pad = _mixer_weights(w_in[l], pool_w[l], dsa_w_uk[l], dsa_w_uv[l])
        (oa, qlatT, c, cT, iqT, ikw, iwT, sbq, sbkT, sbv, swq, swkT, swv) = _proj(
            x2.reshape(B, S, D), row(mix_pre[l]), w_mix, pool_bd, row(pool_scale[l]), row(dsa_kv_norm[l]),
            wuk_bd, tm=tm_proj)
        ob = _dsa(qlatT, iqT, iwT, ikw, cT, c, wuv_pad)
        oc = _sb(sbq, sbkT, sbv)
        od = _swa(swa_sinks[l], swq, swkT, swv)
        W = BRANCH_WIDTH
        x2 = _merge(x2, row(mix_pre[l]), w_gate, oa.reshape(T, W), ob.reshape(T, W), oc.reshape(T, W),
                    od.reshape(T, W), w_branch[l].astype(BF16), w_out[l].astype(BF16), row(mix_post[l]),
                    tm=tm_merge)
        mkT, mv = _memkv(mem, row(mem_norm[l]), xattn_w_kv[l].astype(BF16))
        x3 = _xattn(x2.reshape(B, S, D), row(xattn_pre[l]), xattn_w_q[l].astype(BF16), mkT, mv,
                    xattn_w_o[l].astype(BF16), row(xattn_post[l]), tm=tm_x)
        x = _ffn(x3.reshape(T, D), row(ffn2_pre[l]), ffn2_w_up[l].astype(BF16), ffn2_w_down[l].astype(BF16),
                 row(ffn2_post[l]), tm=tm_ffn, tf=tf).reshape(B, S, D)
    return x
```

```python
import functools

import jax
import jax.numpy as jnp
from jax import lax
from jax.experimental import pallas as pl
from jax.experimental.pallas import tpu as pltpu

F32 = jnp.float32
BF16 = jnp.bfloat16
I32 = jnp.int32
I16 = jnp.int16

RMS_EPS = 1e-6
HEAD_DIM = 64
BRANCH_WIDTH = 256
N_BRANCH = 4
POOL_WINDOWS = (2, 4, 8, 16)
POOL_HALO = 16
DSA_HEADS = 4
DSA_LATENT = 128
IDX_HEADS = 8
IDX_DIM = 32
DSA_TOPK_MAX = 256
SB_HEADS = 4
SWA_HEADS = 4
SWA_KV_HEADS = 2
SWA_WINDOW = 128
MEM_HEADS = 4

BLK = 128
GRP = 4
GK = GRP * BLK
SB_SUB = 2
VMEM_LIMIT = 48 * 1024 * 1024
MASKED = -1e30
INT_MIN = -(2 ** 31)
ONES_ROWS = 16
SCORE_MASKED = -3.3895313892515355e38
KEY_MASKED = (0xFF7F0000 - (1 << 32)) ^ 0x7FFFFFFF
LOG2E = 1.4426950408889634
EXP_ZERO_BELOW = -104.0


def _cparams(sem):
    return pltpu.CompilerParams(dimension_semantics=sem, vmem_limit_bytes=VMEM_LIMIT)


def _rms(x, g):
    return x * lax.rsqrt(jnp.mean(x * x, axis=-1, keepdims=True) + RMS_EPS) * g


def _const_spec(shape):
    nd = len(shape)
    return pl.BlockSpec(shape, lambda *_: (0,) * nd)


def _head_stack(q, heads):
    rows, width = q.shape
    per = width // heads
    lane = lax.broadcasted_iota(I32, (rows, width), 1)
    zero = jnp.zeros_like(q)
    return jnp.concatenate(
        [jnp.where((lane >= h * per) & (lane < (h + 1) * per), q, zero) for h in range(heads)], axis=0)


def _head_merge(o, heads):
    rows = o.shape[0] // heads
    width = o.shape[1]
    per = width // heads
    lane = lax.broadcasted_iota(I32, (rows, width), 1)
    out = jnp.zeros((rows, width), o.dtype)
    for h in range(heads):
        out = jnp.where((lane >= h * per) & (lane < (h + 1) * per), o[h * rows:(h + 1) * rows], out)
    return out


def _ffn_kernel(x_ref, pre_ref, wup_ref, wd_ref, post_ref, o_ref, acc_ref):
    x = x_ref[...]
    hn = _rms(x, pre_ref[...]).astype(BF16)
    nf = wd_ref.shape[0]
    for j in range(nf):
        g = jnp.dot(hn, wup_ref[0, j], preferred_element_type=F32)
        u = jnp.dot(hn, wup_ref[1, j], preferred_element_type=F32)
        a = (g * jax.nn.sigmoid(g) * u).astype(BF16)
        d = jnp.dot(a, wd_ref[j], preferred_element_type=F32)
        if j == 0:
            acc_ref[...] = d
        else:
            acc_ref[...] += d
    o_ref[...] = x + 0.5 * _rms(acc_ref[...], post_ref[...])


def _resident_spec(shape):
    nd = len(shape)
    return pl.BlockSpec(shape, lambda *_: (0,) * nd, pipeline_mode=pl.Buffered(1))


def _ffn(x2, pre, w_up, w_down, post, *, tm, tf):
    T, D = x2.shape
    return pl.pallas_call(
        _ffn_kernel,
        out_shape=jax.ShapeDtypeStruct((T, D), F32),
        grid=(T // tm,),
        in_specs=[
            pl.BlockSpec((tm, D), lambda i: (i, 0)),
            _const_spec((1, D)),
            _resident_spec(w_up.shape),
            _resident_spec(w_down.shape),
            _const_spec((1, D)),
        ],
        out_specs=pl.BlockSpec((tm, D), lambda i: (i, 0)),
        scratch_shapes=[pltpu.VMEM((tm, D), F32)],
        compiler_params=_cparams(("arbitrary",)),
        name="ffn",
    )(x2, pre, w_up, w_down, post)


def _ffn_weights(w_up, w_down, tf):
    D, F2 = w_up.shape
    nf = F2 // 2 // tf
    up = w_up.astype(BF16).reshape(D, 2, nf, tf).transpose(1, 2, 0, 3)
    return up, w_down.astype(BF16).reshape(nf, tf, D)


_SEG = dict(pool=(0, 256), dq=(256, 512), ckv=(512, 640), iq=(640, 896), ikw=(896, 1024),
            sbq=(1024, 1280), sbk=(1280, 1536), sbv=(1536, 1792),
            swq=(1792, 2048), swk=(2048, 2304), swv=(2304, 2560))
_PROJ_W = 2560


def _proj_kernel(x_ref, pre_ref, w_ref, poolw_ref, pools_ref, kvn_ref, wuk_ref,
                 oa_ref, qlatT_ref, c_ref, cT_ref, iqT_ref, ikw_ref, iwT_ref,
                 sbq_ref, sbkT_ref, sbv_ref, swq_ref, swkT_ref, swv_ref, halo_ref):
    si = pl.program_id(1)
    tm = x_ref.shape[1]
    nb = tm // BLK
    hn = _rms(x_ref[0], pre_ref[...]).astype(BF16)
    p = jnp.dot(hn, w_ref[...], preferred_element_type=F32)

    def seg(name):
        a, b = _SEG[name]
        return p[:, a:b]

    @pl.when(si == 0)
    def _():
        halo_ref[...] = jnp.zeros_like(halo_ref)

    u = seg("pool")
    s1 = jnp.concatenate([halo_ref[...], u], axis=0)
    halo_ref[...] = u[tm - POOL_HALO:, :]
    s2 = s1[1:] + s1[:-1]
    s4 = s2[2:] + s2[:-2]
    s8 = s4[4:] + s4[:-4]
    s16 = s8[8:] + s8[:-8]
    lane = lax.broadcasted_iota(I32, (tm, BRANCH_WIDTH), 1)
    t1 = (lax.broadcasted_iota(I32, (tm, BRANCH_WIDTH), 0) + si * tm + 1).astype(F32)
    gw = BRANCH_WIDTH // len(POOL_WINDOWS)
    wsum = jnp.where(lane < gw, s2[-tm:], jnp.where(lane < 2 * gw, s4[-tm:],
                     jnp.where(lane < 3 * gw, s8[-tm:], s16[-tm:])))
    win = jnp.where(lane < gw, float(POOL_WINDOWS[0]), jnp.where(lane < 2 * gw, float(POOL_WINDOWS[1]),
                    jnp.where(lane < 3 * gw, float(POOL_WINDOWS[2]), float(POOL_WINDOWS[3]))))
    pooled = wsum / jnp.minimum(t1, win) - u
    mixed = jnp.dot(pooled.astype(BF16), poolw_ref[...], preferred_element_type=F32)
    oa_ref[0] = (mixed * pools_ref[...]).astype(BF16)

    qlat = jnp.dot(seg("dq").astype(BF16), wuk_ref[...], preferred_element_type=F32) * (HEAD_DIM ** -0.5 * LOG2E)
    c = _rms(seg("ckv"), kvn_ref[...])
    c_ref[0] = c.astype(BF16)
    iq = seg("iq") * IDX_DIM ** -0.5
    ikw = seg("ikw")
    ikw_ref[0] = ikw.astype(BF16)
    sbq_ref[0] = (seg("sbq") * HEAD_DIM ** -0.5).astype(BF16)
    sbv_ref[0] = seg("sbv").astype(BF16)
    swq_ref[0] = (seg("swq") * HEAD_DIM ** -0.5).astype(BF16)
    swv_ref[0] = seg("swv").astype(BF16)
    sbk = seg("sbk")
    swk = seg("swk")
    for g in range(tm // GK):
        cT_ref[0, g, 0:DSA_LATENT, :] = c[g * GK:(g + 1) * GK].T.astype(BF16)
        cT_ref[0, g, DSA_LATENT:, :] = jnp.ones((ONES_ROWS, GK), BF16)
    for b in range(nb):
        rows = slice(b * BLK, (b + 1) * BLK)
        for h in range(DSA_HEADS):
            lanes = slice(h * DSA_LATENT, (h + 1) * DSA_LATENT)
            qlatT_ref[0, b, :, lanes] = qlat[rows, lanes].T.astype(BF16)
        iqT_ref[0, b] = iq[rows].T.astype(BF16)
        iwT_ref[0, b] = ikw[rows].T[IDX_DIM:IDX_DIM + IDX_HEADS] * IDX_HEADS ** -0.5
        sbkT_ref[0, b] = sbk[rows].T.astype(BF16)
        swkT_ref[0, b] = swk[rows].T.astype(BF16)


def _proj(x, pre, w, poolw, pools, kvn, wuk, *, tm):
    B, S, D = x.shape
    nq = S // BLK
    nb = tm // BLK

    def rows(width, dtype):
        return jax.ShapeDtypeStruct((B, S, width), dtype), pl.BlockSpec((1, tm, width), lambda b, s: (b, s, 0))

    def blocks(height, width=BLK, dtype=BF16):
        return (jax.ShapeDtypeStruct((B, nq, height, width), dtype),
                pl.BlockSpec((1, nb, height, width), lambda b, s: (b, s, 0, 0)))

    outs = [rows(256, BF16),
            blocks(DSA_LATENT, DSA_HEADS * BLK),
            rows(128, BF16),
            (jax.ShapeDtypeStruct((B, S // GK, DSA_LATENT + ONES_ROWS, GK), BF16),
             pl.BlockSpec((1, tm // GK, DSA_LATENT + ONES_ROWS, GK), lambda b, s: (b, s, 0, 0))),
            blocks(256),
            rows(128, BF16),
            blocks(IDX_HEADS, BLK, F32),
            rows(256, BF16),
            blocks(256),
            rows(256, BF16),
            rows(256, BF16),
            blocks(256),
            rows(256, BF16)]
    return pl.pallas_call(
        _proj_kernel,
        out_shape=[o[0] for o in outs],
        grid=(B, S // tm),
        in_specs=[pl.BlockSpec((1, tm, D), lambda b, s: (b, s, 0)),
                  _const_spec((1, D)), _const_spec(w.shape), _const_spec(poolw.shape),
                  _const_spec(pools.shape), _const_spec(kvn.shape), _const_spec(wuk.shape)],
        out_specs=[o[1] for o in outs],
        scratch_shapes=[pltpu.VMEM((POOL_HALO, BRANCH_WIDTH), F32)],
        compiler_params=_cparams(("arbitrary", "arbitrary")),
        name="proj",
    )(x, pre, w, poolw, pools, kvn, wuk)


def _sort_key(s):
    b = pltpu.bitcast(s, I32)
    return b ^ ((b >> 31) & 0x7FFFFFFF)


def _dsa_kernel(qlatT_ref, iqT_ref, iwT_ref, ikw_ref, cT_ref, c_ref, wuv_ref, o_ref,
                key_ref, hi_ref, s_ref, acc_ref, *, topk):
    n = pl.program_id(1)
    g_last = n // GRP
    ngrp = g_last + 1
    kf = float(topk)
    krow = lax.broadcasted_iota(I32, (BLK, BLK), 0)
    qcol = lax.broadcasted_iota(I32, (BLK, BLK), 1)

    def rows(g):
        return pl.ds(pl.multiple_of(g * GK, GK), GK)

    iqT = iqT_ref[0, 0]
    zpad = jnp.zeros((BLK - IDX_DIM, BLK), BF16)
    iqw = jnp.concatenate([jnp.concatenate([iqT[h * IDX_DIM:(h + 1) * IDX_DIM], zpad], axis=0)
                           for h in range(IDX_HEADS)], axis=1)
    iwT = iwT_ref[0, 0]
    iwb = [jnp.broadcast_to(iwT[h:h + 1, :], (GK, BLK)) for h in range(IDX_HEADS)]

    def score_group(g):
        kj = ikw_ref[0, rows(g), :]
        acc = None
        for hp in range(IDX_HEADS // 2):
            lg = jnp.dot(kj, iqw[:, hp * 2 * BLK:(hp + 1) * 2 * BLK], preferred_element_type=F32)
            for hh in range(2):
                term = iwb[2 * hp + hh] * jnp.maximum(lg[:, hh * BLK:(hh + 1) * BLK], 0.0)
                acc = term if acc is None else acc + term
        return jnp.where(acc == 0.0, 0.0, acc)

    def store_keys(g, s):
        key = _sort_key(s)
        key_ref[rows(g), :] = key
        hi_ref[rows(g), :] = (key >> 16).astype(I16)

    def score_body(g, carry):
        store_keys(g, score_group(g))
        return carry

    lax.fori_loop(0, g_last, score_body, 0)
    kpos = lax.broadcasted_iota(I32, (GK, BLK), 0) + g_last * GK
    qpos = lax.broadcasted_iota(I32, (GK, BLK), 1) + n * BLK
    store_keys(g_last, jnp.where(kpos <= qpos, score_group(g_last), SCORE_MASKED))

    def count_hi(cand16):
        cb = jnp.broadcast_to(cand16.astype(I16), (BLK, BLK))
        one = jnp.ones((BLK, BLK), I16)
        zero = jnp.zeros((BLK, BLK), I16)

        def body(g, a):
            r0 = pl.multiple_of(g * GK, GK)
            for i in range(GRP):
                a = a + jnp.where(hi_ref[pl.ds(r0 + i * BLK, BLK), :] >= cb, one, zero)
            return a

        a = lax.fori_loop(0, ngrp, body, zero)
        return jnp.sum(a.astype(F32), axis=0, keepdims=True)

    def bisect16(base, cnt_min):
        def step(i, carry):
            t, ct = carry
            cand = t + jnp.left_shift(jnp.int32(1), 15 - i)
            cnt = base + count_hi(cand)
            ok = cnt >= kf
            return jnp.where(ok, cand, t), jnp.where(ok, cnt, ct)

        return lax.fori_loop(0, 16, step, (jnp.full((1, BLK), -(2 ** 15), I32), cnt_min))

    total = jnp.full((1, BLK), 1.0, F32) * (ngrp * GK).astype(F32)
    t16, cnt_thr = bisect16(0.0, total)
    cnt_above = jnp.where(t16 < 2 ** 15 - 1, count_hi(jnp.minimum(t16 + 1, 2 ** 15 - 1)), 0.0)

    t16b = jnp.broadcast_to(t16.astype(I16), (GK, BLK))

    def encode(g, carry):
        lo = ((key_ref[rows(g), :] & 0xFFFF) - 2 ** 15).astype(I16)
        hi_ref[rows(g), :] = jnp.where(hi_ref[rows(g), :] == t16b, lo, jnp.full((GK, BLK), -(2 ** 15), I16))
        return carry

    lax.fori_loop(0, ngrp, encode, 0)
    tlo, cnt_thr = bisect16(cnt_above, cnt_thr)
    thr = jnp.left_shift(t16, 16) + (tlo + 2 ** 15)

    def count_ge(cand):
        cb = jnp.broadcast_to(cand, (BLK, BLK))

        def body(g, a):
            r0 = pl.multiple_of(g * GK, GK)
            for i in range(GRP):
                a = a + jnp.where(key_ref[pl.ds(r0 + i * BLK, BLK), :] >= cb, 1.0, 0.0)
            return a

        a = lax.fori_loop(0, ngrp, body, jnp.zeros((BLK, BLK), F32))
        return jnp.sum(a, axis=0, keepdims=True)

    cnt_gt = count_ge(thr + 1)
    need = kf - cnt_gt
    n_eq = cnt_thr - cnt_gt

    @pl.when(jnp.max(jnp.where(thr > KEY_MASKED, n_eq - need, 0.0)) > 0.0)
    def _():
        tb = jnp.broadcast_to(thr, (BLK, BLK))
        incl = jnp.where(qcol <= krow, 1.0, 0.0).astype(BF16)

        def body(j, run):
            r = pl.ds(pl.multiple_of(j * BLK, BLK), BLK)
            k = key_ref[r, :]
            eq = k == tb
            eqf = jnp.where(eq, 1.0, 0.0)
            pref = jnp.dot(incl, eqf.astype(BF16), preferred_element_type=F32)
            keep = (run + pref) <= need
            key_ref[r, :] = jnp.where(eq & jnp.logical_not(keep), INT_MIN, k)
            return run + jnp.sum(eqf, axis=0, keepdims=True)

        lax.fori_loop(0, n + 1, body, jnp.zeros((1, BLK), F32))

    tsel = jnp.broadcast_to(jnp.maximum(thr, KEY_MASKED + 1), (GK, BLK))

    qsT = qlatT_ref[0, 0]
    W = DSA_HEADS * BLK
    acc_ref[...] = jnp.zeros_like(acc_ref)

    def logits(g, m_run):
        s = jnp.dot(c_ref[0, rows(g), :], qsT, preferred_element_type=F32)
        sel = key_ref[rows(g), :] >= tsel
        s = jnp.where(jnp.concatenate([sel] * DSA_HEADS, axis=1), s, MASKED)
        s_ref[g % 2] = s
        return jnp.maximum(m_run, jnp.max(s, axis=0, keepdims=True))

    def accumulate(g, m_old, m_new):
        alpha = jnp.exp2(m_old - m_new)
        p = jnp.exp2(s_ref[g % 2] - m_new).astype(BF16)
        acc_ref[...] = alpha * acc_ref[...] + jnp.dot(cT_ref[0, g], p, preferred_element_type=F32)

    m_none = jnp.full((1, W), MASKED, F32)

    def attn_body(g, carry):
        m_old, m_new = carry
        accumulate(g, m_old, m_new)
        return m_new, logits(g + 1, m_new)

    m_old, m_new = lax.fori_loop(0, g_last, attn_body, (m_none, logits(0, m_none)))
    accumulate(g_last, m_old, m_new)
    l = acc_ref[DSA_LATENT:DSA_LATENT + 1, :]
    olatT = acc_ref[0:DSA_LATENT, :] * (1.0 / l)
    out = None
    for h in range(DSA_HEADS):
        olat = olatT[:, h * BLK:(h + 1) * BLK].T.astype(BF16)
        y = jnp.dot(olat, wuv_ref[h], preferred_element_type=F32)
        out = y if out is None else out + y
    o_ref[0] = out.astype(BF16)


def _dsa(qlatT, iqT, iwT, ikw, cT, c, wuv):
    B, S, _ = c.shape
    assert S % GK == 0
    topk = min(DSA_TOPK_MAX, S // 4)
    blk4 = lambda a: pl.BlockSpec((1, 1) + a.shape[2:], lambda b, n: (b, n, 0, 0))
    return pl.pallas_call(
        functools.partial(_dsa_kernel, topk=topk),
        out_shape=jax.ShapeDtypeStruct((B, S, BRANCH_WIDTH), BF16),
        grid=(B, S // BLK),
        in_specs=[blk4(qlatT), blk4(iqT), blk4(iwT),
                  pl.BlockSpec((1,) + ikw.shape[1:], lambda b, n: (b, 0, 0)),
                  pl.BlockSpec((1,) + cT.shape[1:], lambda b, n: (b, 0, 0, 0)),
                  pl.BlockSpec((1,) + c.shape[1:], lambda b, n: (b, 0, 0)),
                  _const_spec(wuv.shape)],
        out_specs=pl.BlockSpec((1, BLK, BRANCH_WIDTH), lambda b, n: (b, n, 0)),
        scratch_shapes=[pltpu.VMEM((S, BLK), I32), pltpu.VMEM((S, BLK), I16),
                        pltpu.VMEM((2, GK, DSA_HEADS * BLK), F32),
                        pltpu.VMEM((DSA_LATENT + ONES_ROWS, DSA_HEADS * BLK), F32)],
        compiler_params=_cparams(("arbitrary", "arbitrary")),
        name="dsa",
    )(qlatT, iqT, iwT, ikw, cT, c, wuv)


def _sb_kernel(q_ref, kT_ref, v_ref, o_ref, acc_ref):
    n = pl.program_id(1)
    H = SB_HEADS
    qm = [_head_stack(q_ref[0, i * BLK:(i + 1) * BLK, :], H) for i in range(SB_SUB)]
    row = lax.broadcasted_iota(I32, (BLK, BLK), 0)
    col = lax.broadcasted_iota(I32, (BLK, BLK), 1)
    after = jnp.where(row > col, 1.0, 0.0).astype(BF16)
    after2 = jnp.concatenate([after, after], axis=0)
    strict = jnp.concatenate([col < row] * H, axis=0)

    def step(i, j, run, diag):
        z = jnp.dot(qm[i], kT_ref[0, j], preferred_element_type=F32)
        t = jnp.log(1.0 + jnp.exp(-jnp.abs(z)))
        l1m = -(jnp.maximum(z, 0.0) + t)
        lsg = jnp.minimum(z, 0.0) - t
        if diag:
            l1m = jnp.where(strict, l1m, 0.0)
        hi = l1m.astype(BF16)
        r1 = l1m - hi.astype(F32)
        mid = r1.astype(BF16)
        lo = (r1 - mid.astype(F32)).astype(BF16)
        suf = (jnp.dot(jnp.concatenate([hi, mid], axis=1), after2, preferred_element_type=F32)
               + jnp.dot(lo, after, preferred_element_type=F32))
        a = jnp.exp(lsg + run + suf)
        if diag:
            a = jnp.where(strict, a, 0.0)
        vj = v_ref[0, pl.ds(pl.multiple_of(j * BLK, BLK), BLK), :]
        pv = jnp.dot(a.astype(BF16), vj, preferred_element_type=F32)
        if diag:
            acc_ref[i] = pv
        else:
            acc_ref[i] += pv
        return run + jnp.sum(l1m, axis=1, keepdims=True)

    def any_alive(runs):
        m = jnp.max(runs[0])
        for r in runs[1:]:
            m = jnp.maximum(m, jnp.max(r))
        return m > EXP_ZERO_BELOW

    first = n * SB_SUB
    runs0 = tuple(step(i, first + i, jnp.zeros((H * BLK, 1), F32), True) for i in range(SB_SUB))

    def cond(carry):
        t, _, alive = carry
        return jnp.logical_and(first + SB_SUB - 1 - t >= 0, alive)

    def body(carry):
        t, runs, _ = carry
        new = []
        for i in range(SB_SUB):
            j = first + i - t
            run = jnp.where(j >= 0, runs[i], MASKED)
            new.append(step(i, jnp.maximum(j, 0), run, False))
        return t + 1, tuple(new), any_alive(new)

    lax.while_loop(cond, body, (jnp.int32(1), runs0, any_alive(runs0)))
    for i in range(SB_SUB):
        o_ref[0, i * BLK:(i + 1) * BLK, :] = _head_merge(acc_ref[i], H).astype(BF16)


def _sb(q, kT, v):
    B, S, W = q.shape
    tq = SB_SUB * BLK
    return pl.pallas_call(
        _sb_kernel,
        out_shape=jax.ShapeDtypeStruct((B, S, W), BF16),
        grid=(B, S // tq),
        in_specs=[pl.BlockSpec((1, tq, W), lambda b, n: (b, n, 0)),
                  pl.BlockSpec((1,) + kT.shape[1:], lambda b, n: (b, 0, 0, 0)),
                  pl.BlockSpec((1,) + v.shape[1:], lambda b, n: (b, 0, 0))],
        out_specs=pl.BlockSpec((1, tq, W), lambda b, n: (b, n, 0)),
        scratch_shapes=[pltpu.VMEM((SB_SUB, SB_HEADS * BLK, W), F32)],
        compiler_params=_cparams(("arbitrary", "arbitrary")),
        name="sb",
    )(q, kT, v)


def _swa_kernel(sink_ref, q_ref, kTp_ref, kTc_ref, vp_ref, vc_ref, o_ref):
    n = pl.program_id(1)
    H = SWA_HEADS
    nsub = q_ref.shape[1] // BLK
    r = lax.broadcasted_iota(I32, (BLK, 2 * BLK), 0)
    c = lax.broadcasted_iota(I32, (BLK, 2 * BLK), 1)
    rel = r + BLK - c
    in_window = (rel >= 0) & (rel < SWA_WINDOW)
    sink = jnp.concatenate([jnp.full((BLK, 1), sink_ref[h], F32) for h in range(H)], axis=0)
    for i in range(nsub):
        rows = slice(i * BLK, (i + 1) * BLK)
        qm = _head_stack(q_ref[0, rows, :], H)
        kT_prev = kTp_ref[0, 0] if i == 0 else kTc_ref[0, i - 1]
        v_prev = vp_ref[0] if i == 0 else vc_ref[0, (i - 1) * BLK:i * BLK, :]
        kT = jnp.concatenate([kT_prev, kTc_ref[0, i]], axis=1)
        s = jnp.dot(qm, kT, preferred_element_type=F32)
        ok = in_window & ((c >= BLK) | (n > 0)) if i == 0 else in_window
        s = jnp.where(jnp.concatenate([ok] * H, axis=0), s, MASKED)
        m = jnp.maximum(jnp.max(s, axis=1, keepdims=True), sink)
        p = jnp.exp(s - m)
        denom = jnp.sum(p, axis=1, keepdims=True) + jnp.exp(sink - m)
        p = (p / denom).astype(BF16)
        v = jnp.concatenate([v_prev, vc_ref[0, rows, :]], axis=0)
        o_ref[0, rows, :] = _head_merge(jnp.dot(p, v, preferred_element_type=F32), H).astype(BF16)


def _swa(sinks, q, kT, v, *, tq):
    B, S, W = q.shape
    nsub = tq // BLK
    prev = lambda b, n: (b, jnp.maximum(n * nsub - 1, 0), 0, 0)
    prev3 = lambda b, n: (b, jnp.maximum(n * nsub - 1, 0), 0)
    return pl.pallas_call(
        _swa_kernel,
        out_shape=jax.ShapeDtypeStruct((B, S, W), BF16),
        grid=(B, S // tq),
        in_specs=[pl.BlockSpec(memory_space=pltpu.SMEM),
                  pl.BlockSpec((1, tq, W), lambda b, n: (b, n, 0)),
                  pl.BlockSpec((1, 1, W, BLK), prev),
                  pl.BlockSpec((1, nsub, W, BLK), lambda b, n: (b, n, 0, 0)),
                  pl.BlockSpec((1, BLK, W), prev3),
                  pl.BlockSpec((1, tq, W), lambda b, n: (b, n, 0))],
        out_specs=pl.BlockSpec((1, tq, W), lambda b, n: (b, n, 0)),
        compiler_params=_cparams(("arbitrary", "arbitrary")),
        name="swa",
    )(sinks, q, kT, kT, v, v)


def _merge_kernel(x_ref, pre_ref, wg_ref, oa_ref, ob_ref, oc_ref, od_ref, wb_ref, wo_ref, post_ref, o_ref):
    x = x_ref[...]
    D = x.shape[1]
    hn = _rms(x, pre_ref[...]).astype(BF16)
    merged = None
    for i, br in enumerate((oa_ref, ob_ref, oc_ref, od_ref)):
        gate = jax.nn.sigmoid(jnp.dot(hn, wg_ref[:, i * D:(i + 1) * D], preferred_element_type=F32))
        y = jnp.dot(br[...], wb_ref[i], preferred_element_type=F32)
        merged = gate * y if merged is None else merged + gate * y
    m = jnp.dot(merged.astype(BF16), wo_ref[...], preferred_element_type=F32)
    o_ref[...] = x + _rms(m, post_ref[...])


def _merge(x2, pre, wg, oa, ob, oc, od, wb, wo, post, *, tm):
    T, D = x2.shape
    W = oa.shape[1]
    br = pl.BlockSpec((tm, W), lambda i: (i, 0))
    return pl.pallas_call(
        _merge_kernel,
        out_shape=jax.ShapeDtypeStruct((T, D), F32),
        grid=(T // tm,),
        in_specs=[pl.BlockSpec((tm, D), lambda i: (i, 0)), _const_spec((1, D)), _const_spec(wg.shape),
                  br, br, br, br, _const_spec(wb.shape), _const_spec(wo.shape), _const_spec((1, D))],
        out_specs=pl.BlockSpec((tm, D), lambda i: (i, 0)),
        compiler_params=_cparams(("arbitrary",)),
        name="merge",
    )(x2, pre, wg, oa, ob, oc, od, wb, wo, post)


def _memkv_kernel(mem_ref, g_ref, wkv_ref, kT_ref, v_ref):
    mn = _rms(mem_ref[0], g_ref[...]).astype(BF16)
    kv = jnp.dot(mn, wkv_ref[...], preferred_element_type=F32)
    W = kv.shape[1] // 2
    kT_ref[0] = kv[:, :W].T.astype(BF16)
    v_ref[0] = kv[:, W:].astype(BF16)


def _memkv(mem, g, wkv):
    B, M, D = mem.shape
    W = wkv.shape[1] // 2
    return pl.pallas_call(
        _memkv_kernel,
        out_shape=[jax.ShapeDtypeStruct((B, W, M), BF16), jax.ShapeDtypeStruct((B, M, W), BF16)],
        grid=(B,),
        in_specs=[pl.BlockSpec((1, M, D), lambda b: (b, 0, 0)), _const_spec((1, D)), _const_spec(wkv.shape)],
        out_specs=[pl.BlockSpec((1, W, M), lambda b: (b, 0, 0)), pl.BlockSpec((1, M, W), lambda b: (b, 0, 0))],
        compiler_params=_cparams(("arbitrary",)),
        name="memkv",
    )(mem, g, wkv)


def _xattn_kernel(x_ref, pre_ref, wq_ref, kT_ref, v_ref, wo_ref, post_ref, o_ref):
    x = x_ref[0]
    hn = _rms(x, pre_ref[...]).astype(BF16)
    q = (jnp.dot(hn, wq_ref[...], preferred_element_type=F32) * HEAD_DIM ** -0.5).astype(BF16)
    s = jnp.dot(_head_stack(q, MEM_HEADS), kT_ref[0], preferred_element_type=F32)
    p = jnp.exp(s - jnp.max(s, axis=1, keepdims=True))
    p = (p / jnp.sum(p, axis=1, keepdims=True)).astype(BF16)
    o = _head_merge(jnp.dot(p, v_ref[0], preferred_element_type=F32), MEM_HEADS)
    c = jnp.dot(o.astype(BF16), wo_ref[...], preferred_element_type=F32)
    o_ref[0] = x + _rms(c, post_ref[...])


def _xattn(x, pre, wq, kT, v, wo, post, *, tm):
    B, S, D = x.shape
    return pl.pallas_call(
        _xattn_kernel,
        out_shape=jax.ShapeDtypeStruct((B, S, D), F32),
        grid=(B, S // tm),
        in_specs=[pl.BlockSpec((1, tm, D), lambda b, s: (b, s, 0)), _const_spec((1, D)), _const_spec(wq.shape),
                  pl.BlockSpec((1,) + kT.shape[1:], lambda b, s: (b, 0, 0)),
                  pl.BlockSpec((1,) + v.shape[1:], lambda b, s: (b, 0, 0)),
                  _const_spec(wo.shape), _const_spec((1, D))],
        out_specs=pl.BlockSpec((1, tm, D), lambda b, s: (b, s, 0)),
        compiler_params=_cparams(("arbitrary", "arbitrary")),
        name="xattn",
    )(x, pre, wq, kT, v, wo, post)


def _mixer_weights(w_in, pool_w, dsa_w_uk, dsa_w_uv):
    D = w_in.shape[0]
    sizes = (BRANCH_WIDTH, DSA_HEADS * HEAD_DIM, DSA_LATENT, IDX_HEADS * IDX_DIM, IDX_DIM, IDX_HEADS,
             SB_HEADS * HEAD_DIM, SB_HEADS * HEAD_DIM, SB_HEADS * HEAD_DIM,
             SWA_HEADS * HEAD_DIM, SWA_KV_HEADS * HEAD_DIM, SWA_KV_HEADS * HEAD_DIM)
    cols, acc = [], 0
    for s in sizes:
        cols.append(w_in[:, acc:acc + s])
        acc += s
    (w_pool, w_dq, w_ckv, w_iq, w_ik, w_iw, w_sbq, w_sbk, w_sbv, w_swq, w_swk, w_swv) = cols
    w_gate = w_in[:, acc:]
    group = SWA_HEADS // SWA_KV_HEADS

    def expand_kv(w):
        return jnp.repeat(w.reshape(D, SWA_KV_HEADS, HEAD_DIM), group, axis=1).reshape(D, SWA_HEADS * HEAD_DIM)

    ikw_pad = jnp.zeros((D, 128 - IDX_DIM - IDX_HEADS), w_in.dtype)
    w_mix = jnp.concatenate([w_pool, w_dq, w_ckv, w_iq, w_ik, w_iw, ikw_pad, w_sbq, w_sbk, w_sbv,
                             w_swq, expand_kv(w_swk), expand_kv(w_swv)], axis=1).astype(BF16)
    gd = pool_w.shape[1]
    pool_bd = jnp.zeros((BRANCH_WIDTH, BRANCH_WIDTH), F32)
    for g in range(pool_w.shape[0]):
        pool_bd = pool_bd.at[g * gd:(g + 1) * gd, g * gd:(g + 1) * gd].set(pool_w[g])
    wuk_bd = jnp.zeros((DSA_HEADS * HEAD_DIM, DSA_HEADS * DSA_LATENT), F32)
    wuv_pad = jnp.zeros((DSA_HEADS, DSA_LATENT, DSA_HEADS * HEAD_DIM), F32)
    for h in range(DSA_HEADS):
        wuk_bd = wuk_bd.at[h * HEAD_DIM:(h + 1) * HEAD_DIM, h * DSA_LATENT:(h + 1) * DSA_LATENT].set(dsa_w_uk[:, h, :].T)
        wuv_pad = wuv_pad.at[h, :, h * HEAD_DIM:(h + 1) * HEAD_DIM].set(dsa_w_uv[:, h, :])
    return w_mix, w_gate.astype(BF16), pool_bd.astype(BF16), wuk_bd.astype(BF16), wuv_pad.astype(BF16)


def kernel(x, mem, ffn1_pre, ffn1_w_up, ffn1_w_down, ffn1_post, mix_pre, w_in, pool_w, pool_scale, dsa_kv_norm, dsa_w_uk, dsa_w_uv, swa_sinks, w_branch, w_out, mix_post, xattn_pre, mem_norm, xattn_w_q, xattn_w_kv, xattn_w_o, xattn_post, ffn2_pre, ffn2_w_up, ffn2_w_down, ffn2_post):
    B, S, D = x.shape
    depth = w_in.shape[0]
    T = B * S
    tm_ffn = min(1024, T)
    tf = 256
    tm_proj = min(512, S)
    tm_merge = min(512, T)
    tm_x = min(512, S)
    row = lambda a: a.reshape(1, -1)
    for l in range(depth):
        x2 = _ffn(x.reshape(T, D), row(ffn1_pre[l]), *_ffn_weights(ffn1_w_up[l], ffn1_w_down[l], tf),
                  row(ffn1_post[l]), tm=tm_ffn, tf=tf)
        w_mix, w_gate, pool_bd, wuk_bd, wuv_pad = _mixer_weights(w_in[l], pool_w[l], dsa_w_uk[l], dsa_w_uv[l])
        (oa, qlatT, c, cT, iqT, ikw, iwT, sbq, sbkT, sbv, swq, swkT, swv) = _proj(
            x2.reshape(B, S, D), row(mix_pre[l]), w_mix, pool_bd, row(pool_scale[l]), row(dsa_kv_norm[l]),
            wuk_bd, tm=tm_proj)
        ob = _dsa(qlatT, iqT, iwT, ikw, cT, c, wuv_pad)
        oc = _sb(sbq, sbkT, sbv)
        od = _swa(swa_sinks[l], swq, swkT, swv, tq=min(512, S))
        W = BRANCH_WIDTH
        x2 = _merge(x2, row(mix_pre[l]), w_gate, oa.reshape(T, W), ob.reshape(T, W), oc.reshape(T, W),
                    od.reshape(T, W), w_branch[l].astype(BF16), w_out[l].astype(BF16), row(mix_post[l]),
                    tm=tm_merge)
        mkT, mv = _memkv(mem, row(mem_norm[l]), xattn_w_kv[l].astype(BF16))
        x3 = _xattn(x2.reshape(B, S, D), row(xattn_pre[l]), xattn_w_q[l].astype(BF16), mkT, mv,
                    xattn_w_o[l].astype(BF16), row(xattn_post[l]), tm=tm_x)
        x = _ffn(x3.reshape(T, D), row(ffn2_pre[l]), *_ffn_weights(ffn2_w_up[l], ffn2_w_down[l], tf),
                 row(ffn2_post[l]), tm=tm_ffn, tf=tf).reshape(B, S, D)
    return x
```

```python
import functools

import jax
import jax.numpy as jnp
from jax import lax
from jax.experimental import pallas as pl
from jax.experimental.pallas import tpu as pltpu

F32 = jnp.float32
BF16 = jnp.bfloat16
I32 = jnp.int32

RMS_EPS = 1e-6
HEAD_DIM = 64
BRANCH_WIDTH = 256
N_BRANCH = 4
POOL_WINDOWS = (2, 4, 8, 16)
POOL_HALO = 16
DSA_HEADS = 4
DSA_LATENT = 128
IDX_HEADS = 8
IDX_DIM = 32
DSA_TOPK_MAX = 256
SB_HEADS = 4
SWA_HEADS = 4
SWA_KV_HEADS = 2
SWA_WINDOW = 128
MEM_HEADS = 4

BLK = 128
GRP = 4
GK = GRP * BLK
SB_SUB = 2
VMEM_LIMIT = 48 * 1024 * 1024
MASKED = -1e30
INT_MIN = -(2 ** 31)
COARSE_BITS = 15
N_CAND = 4
CAND_ROWS = 32
ONES_ROWS = 16
SCORE_MASKED = -3.3895313892515355e38
KEY_MASKED = (0xFF7F0000 - (1 << 32)) ^ 0x7FFFFFFF
LOG2E = 1.4426950408889634
EXP_ZERO_BELOW = -104.0


def _cparams(sem):
    return pltpu.CompilerParams(dimension_semantics=sem, vmem_limit_bytes=VMEM_LIMIT)


def _rms(x, g):
    return x * lax.rsqrt(jnp.mean(x * x, axis=-1, keepdims=True) + RMS_EPS) * g


def _const_spec(shape):
    nd = len(shape)
    return pl.BlockSpec(shape, lambda *_: (0,) * nd)


def _head_stack(q, heads):
    rows, width = q.shape
    per = width // heads
    lane = lax.broadcasted_iota(I32, (rows, width), 1)
    zero = jnp.zeros_like(q)
    return jnp.concatenate(
        [jnp.where((lane >= h * per) & (lane < (h + 1) * per), q, zero) for h in range(heads)], axis=0)


def _head_merge(o, heads):
    rows = o.shape[0] // heads
    width = o.shape[1]
    per = width // heads
    lane = lax.broadcasted_iota(I32, (rows, width), 1)
    out = jnp.zeros((rows, width), o.dtype)
    for h in range(heads):
        out = jnp.where((lane >= h * per) & (lane < (h + 1) * per), o[h * rows:(h + 1) * rows], out)
    return out


def _ffn_kernel(x_ref, pre_ref, wup_ref, wd_ref, post_ref, o_ref, acc_ref):
    x = x_ref[...]
    hn = _rms(x, pre_ref[...]).astype(BF16)
    nf = wd_ref.shape[0]
    for j in range(nf):
        g = jnp.dot(hn, wup_ref[0, j], preferred_element_type=F32)
        u = jnp.dot(hn, wup_ref[1, j], preferred_element_type=F32)
        a = (g * jax.nn.sigmoid(g) * u).astype(BF16)
        d = jnp.dot(a, wd_ref[j], preferred_element_type=F32)
        if j == 0:
            acc_ref[...] = d
        else:
            acc_ref[...] += d
    o_ref[...] = x + 0.5 * _rms(acc_ref[...], post_ref[...])


def _resident_spec(shape):
    nd = len(shape)
    return pl.BlockSpec(shape, lambda *_: (0,) * nd, pipeline_mode=pl.Buffered(1))


def _ffn(x2, pre, w_up, w_down, post, *, tm, tf):
    T, D = x2.shape
    return pl.pallas_call(
        _ffn_kernel,
        out_shape=jax.ShapeDtypeStruct((T, D), F32),
        grid=(T // tm,),
        in_specs=[
            pl.BlockSpec((tm, D), lambda i: (i, 0)),
            _const_spec((1, D)),
            _resident_spec(w_up.shape),
            _resident_spec(w_down.shape),
            _const_spec((1, D)),
        ],
        out_specs=pl.BlockSpec((tm, D), lambda i: (i, 0)),
        scratch_shapes=[pltpu.VMEM((tm, D), F32)],
        compiler_params=_cparams(("arbitrary",)),
        name="ffn",
    )(x2, pre, w_up, w_down, post)


def _ffn_weights(w_up, w_down, tf):
    D, F2 = w_up.shape
    nf = F2 // 2 // tf
    up = w_up.astype(BF16).reshape(D, 2, nf, tf).transpose(1, 2, 0, 3)
    return up, w_down.astype(BF16).reshape(nf, tf, D)


_SEG = dict(pool=(0, 256), dq=(256, 512), ckv=(512, 640), iq=(640, 896), ikw=(896, 1024),
            sbq=(1024, 1280), sbk=(1280, 1536), sbv=(1536, 1792),
            swq=(1792, 2048), swk=(2048, 2304), swv=(2304, 2560))
_PROJ_W = 2560


def _proj_kernel(x_ref, pre_ref, w_ref, poolw_ref, pools_ref, kvn_ref, wuk_ref,
                 oa_ref, qlatT_ref, c_ref, cT_ref, iqT_ref, ikw_ref, iwT_ref,
                 sbq_ref, sbkT_ref, sbv_ref, swq_ref, swkT_ref, swv_ref, halo_ref):
    si = pl.program_id(1)
    tm = x_ref.shape[1]
    nb = tm // BLK
    hn = _rms(x_ref[0], pre_ref[...]).astype(BF16)
    p = jnp.dot(hn, w_ref[...], preferred_element_type=F32)

    def seg(name):
        a, b = _SEG[name]
        return p[:, a:b]

    @pl.when(si == 0)
    def _():
        halo_ref[...] = jnp.zeros_like(halo_ref)

    u = seg("pool")
    s1 = jnp.concatenate([halo_ref[...], u], axis=0)
    halo_ref[...] = u[tm - POOL_HALO:, :]
    s2 = s1[1:] + s1[:-1]
    s4 = s2[2:] + s2[:-2]
    s8 = s4[4:] + s4[:-4]
    s16 = s8[8:] + s8[:-8]
    lane = lax.broadcasted_iota(I32, (tm, BRANCH_WIDTH), 1)
    t1 = (lax.broadcasted_iota(I32, (tm, BRANCH_WIDTH), 0) + si * tm + 1).astype(F32)
    gw = BRANCH_WIDTH // len(POOL_WINDOWS)
    wsum = jnp.where(lane < gw, s2[-tm:], jnp.where(lane < 2 * gw, s4[-tm:],
                     jnp.where(lane < 3 * gw, s8[-tm:], s16[-tm:])))
    win = jnp.where(lane < gw, float(POOL_WINDOWS[0]), jnp.where(lane < 2 * gw, float(POOL_WINDOWS[1]),
                    jnp.where(lane < 3 * gw, float(POOL_WINDOWS[2]), float(POOL_WINDOWS[3]))))
    pooled = wsum / jnp.minimum(t1, win) - u
    mixed = jnp.dot(pooled.astype(BF16), poolw_ref[...], preferred_element_type=F32)
    oa_ref[0] = (mixed * pools_ref[...]).astype(BF16)

    qlat = jnp.dot(seg("dq").astype(BF16), wuk_ref[...], preferred_element_type=F32) * (HEAD_DIM ** -0.5 * LOG2E)
    c = _rms(seg("ckv"), kvn_ref[...])
    c_ref[0] = c.astype(BF16)
    iq = seg("iq") * IDX_DIM ** -0.5
    ikw = seg("ikw")
    ikw_ref[0] = ikw.astype(BF16)
    sbq_ref[0] = (seg("sbq") * HEAD_DIM ** -0.5).astype(BF16)
    sbv_ref[0] = seg("sbv").astype(BF16)
    swq_ref[0] = (seg("swq") * HEAD_DIM ** -0.5).astype(BF16)
    swv_ref[0] = seg("swv").astype(BF16)
    sbk = seg("sbk")
    swk = seg("swk")
    for g in range(tm // GK):
        cT_ref[0, g, 0:DSA_LATENT, :] = c[g * GK:(g + 1) * GK].T.astype(BF16)
        cT_ref[0, g, DSA_LATENT:, :] = jnp.ones((ONES_ROWS, GK), BF16)
    for b in range(nb):
        rows = slice(b * BLK, (b + 1) * BLK)
        for h in range(DSA_HEADS):
            lanes = slice(h * DSA_LATENT, (h + 1) * DSA_LATENT)
            qlatT_ref[0, b, :, lanes] = qlat[rows, lanes].T.astype(BF16)
        iqT_ref[0, b] = iq[rows].T.astype(BF16)
        iwT_ref[0, b] = ikw[rows].T[IDX_DIM:IDX_DIM + IDX_HEADS] * IDX_HEADS ** -0.5
        sbkT_ref[0, b] = sbk[rows].T.astype(BF16)
        swkT_ref[0, b] = swk[rows].T.astype(BF16)


def _proj(x, pre, w, poolw, pools, kvn, wuk, *, tm):
    B, S, D = x.shape
    nq = S // BLK
    nb = tm // BLK

    def rows(width, dtype):
        return jax.ShapeDtypeStruct((B, S, width), dtype), pl.BlockSpec((1, tm, width), lambda b, s: (b, s, 0))

    def blocks(height, width=BLK, dtype=BF16):
        return (jax.ShapeDtypeStruct((B, nq, height, width), dtype),
                pl.BlockSpec((1, nb, height, width), lambda b, s: (b, s, 0, 0)))

    outs = [rows(256, BF16),
            blocks(DSA_LATENT, DSA_HEADS * BLK),
            rows(128, BF16),
            (jax.ShapeDtypeStruct((B, S // GK, DSA_LATENT + ONES_ROWS, GK), BF16),
             pl.BlockSpec((1, tm // GK, DSA_LATENT + ONES_ROWS, GK), lambda b, s: (b, s, 0, 0))),
            blocks(256),
            rows(128, BF16),
            blocks(IDX_HEADS, BLK, F32),
            rows(256, BF16),
            blocks(256),
            rows(256, BF16),
            rows(256, BF16),
            blocks(256),
            rows(256, BF16)]
    return pl.pallas_call(
        _proj_kernel,
        out_shape=[o[0] for o in outs],
        grid=(B, S // tm),
        in_specs=[pl.BlockSpec((1, tm, D), lambda b, s: (b, s, 0)),
                  _const_spec((1, D)), _const_spec(w.shape), _const_spec(poolw.shape),
                  _const_spec(pools.shape), _const_spec(kvn.shape), _const_spec(wuk.shape)],
        out_specs=[o[1] for o in outs],
        scratch_shapes=[pltpu.VMEM((POOL_HALO, BRANCH_WIDTH), F32)],
        compiler_params=_cparams(("arbitrary", "arbitrary")),
        name="proj",
    )(x, pre, w, poolw, pools, kvn, wuk)


def _sort_key(s):
    b = pltpu.bitcast(s, I32)
    return b ^ ((b >> 31) & 0x7FFFFFFF)


def _dsa_kernel(qlatT_ref, iqT_ref, iwT_ref, ikw_ref, cT_ref, c_ref, wuv_ref, o_ref,
                key_ref, cand_ref, s_ref, acc_ref, *, topk):
    n = pl.program_id(1)
    g_last = n // GRP
    ngrp = g_last + 1
    kf = float(topk)
    krow = lax.broadcasted_iota(I32, (BLK, BLK), 0)
    qcol = lax.broadcasted_iota(I32, (BLK, BLK), 1)

    def rows(g):
        return pl.ds(pl.multiple_of(g * GK, GK), GK)

    iqT = iqT_ref[0, 0]
    zpad = jnp.zeros((BLK - IDX_DIM, BLK), BF16)
    iqw = jnp.concatenate([jnp.concatenate([iqT[h * IDX_DIM:(h + 1) * IDX_DIM], zpad], axis=0)
                           for h in range(IDX_HEADS)], axis=1)
    iwT = iwT_ref[0, 0]
    iwb = [jnp.broadcast_to(iwT[h:h + 1, :], (GK, BLK)) for h in range(IDX_HEADS)]

    def score_group(g):
        kj = ikw_ref[0, rows(g), :]
        acc = None
        for hp in range(IDX_HEADS // 2):
            lg = jnp.dot(kj, iqw[:, hp * 2 * BLK:(hp + 1) * 2 * BLK], preferred_element_type=F32)
            for hh in range(2):
                term = iwb[2 * hp + hh] * jnp.maximum(lg[:, hh * BLK:(hh + 1) * BLK], 0.0)
                acc = term if acc is None else acc + term
        return jnp.where(acc == 0.0, 0.0, acc)

    def score_body(g, carry):
        key_ref[rows(g), :] = _sort_key(score_group(g))
        return carry

    lax.fori_loop(0, g_last, score_body, 0)
    kpos = lax.broadcasted_iota(I32, (GK, BLK), 0) + g_last * GK
    qpos = lax.broadcasted_iota(I32, (GK, BLK), 1) + n * BLK
    key_ref[rows(g_last), :] = _sort_key(jnp.where(kpos <= qpos, score_group(g_last), SCORE_MASKED))

    def count_ge(cand):
        cb = jnp.broadcast_to(cand, (BLK, BLK))

        def body(g, a):
            r0 = pl.multiple_of(g * GK, GK)
            for i in range(GRP):
                a = a + jnp.where(key_ref[pl.ds(r0 + i * BLK, BLK), :] >= cb, 1.0, 0.0)
            return a

        a = lax.fori_loop(0, ngrp, body, jnp.zeros((BLK, BLK), F32))
        return jnp.sum(a, axis=0, keepdims=True)

    def bisect(count_fn, top_bit, nbits, carry):
        def step(i, c):
            t, c_t, c_up = c
            cand = t + jnp.left_shift(jnp.int32(1), top_bit - i)
            cnt = count_fn(cand)
            ok = cnt >= kf
            return jnp.where(ok, cand, t), jnp.where(ok, cnt, c_t), jnp.where(ok, c_up, cnt)

        return lax.fori_loop(0, nbits, step, carry)

    total = jnp.full((1, BLK), 1.0, F32) * (ngrp * GK).astype(F32)
    cnt0 = count_ge(jnp.zeros((1, BLK), I32))
    ok0 = cnt0 >= kf
    coarse = bisect(count_ge, 30, COARSE_BITS - 1,
                    (jnp.where(ok0, 0, INT_MIN).astype(I32), jnp.where(ok0, cnt0, total), jnp.where(ok0, 0.0, cnt0)))
    t_c, _, c_up_c = coarse
    fine_bits = 32 - COARSE_BITS

    lo_b = jnp.broadcast_to(t_c, (CAND_ROWS, BLK))
    hi_b = jnp.broadcast_to(t_c + (2 ** fine_bits - 1), (CAND_ROWS, BLK))
    empty = jnp.full((CAND_ROWS, BLK), INT_MIN, I32)
    spill = empty
    for p in range(BLK // CAND_ROWS):
        def gather(g, carry, p=p):
            kept, dropped = list(carry[:N_CAND]), carry[N_CAND]
            r0 = pl.multiple_of(g * GK, GK)
            for i in range(GRP):
                x = key_ref[pl.ds(r0 + i * BLK + p * CAND_ROWS, CAND_ROWS), :]
                x = jnp.where(x >= lo_b, x, INT_MIN)
                x = jnp.where(x <= hi_b, x, INT_MIN)
                for lvl in range(N_CAND):
                    kept[lvl], x = jnp.maximum(kept[lvl], x), jnp.minimum(kept[lvl], x)
                dropped = jnp.maximum(dropped, x)
            return (*kept, dropped)

        res = lax.fori_loop(0, ngrp, gather, (empty,) * N_CAND + (spill,))
        for lvl in range(N_CAND):
            cand_ref[lvl, p * CAND_ROWS:(p + 1) * CAND_ROWS, :] = res[lvl]
        spill = res[N_CAND]

    def count_cand(cand):
        cb = jnp.broadcast_to(cand, (BLK, BLK))
        a = jnp.where(cand_ref[0] >= cb, 1.0, 0.0)
        for lvl in range(1, N_CAND):
            a = a + jnp.where(cand_ref[lvl] >= cb, 1.0, 0.0)
        return c_up_c + jnp.sum(a, axis=0, keepdims=True)

    thr, cnt_thr, cnt_gt = lax.cond(jnp.max(spill) > INT_MIN,
                                    lambda: bisect(count_ge, fine_bits - 1, fine_bits, coarse),
                                    lambda: bisect(count_cand, fine_bits - 1, fine_bits, coarse))
    need = kf - cnt_gt
    n_eq = cnt_thr - cnt_gt

    @pl.when(jnp.max(jnp.where(thr > KEY_MASKED, n_eq - need, 0.0)) > 0.0)
    def _():
        tb = jnp.broadcast_to(thr, (BLK, BLK))
        incl = jnp.where(qcol <= krow, 1.0, 0.0).astype(BF16)

        def body(j, run):
            r = pl.ds(pl.multiple_of(j * BLK, BLK), BLK)
            k = key_ref[r, :]
            eq = k == tb
            eqf = jnp.where(eq, 1.0, 0.0)
            pref = jnp.dot(incl, eqf.astype(BF16), preferred_element_type=F32)
            keep = (run + pref) <= need
            key_ref[r, :] = jnp.where(eq & jnp.logical_not(keep), INT_MIN, k)
            return run + jnp.sum(eqf, axis=0, keepdims=True)

        lax.fori_loop(0, n + 1, body, jnp.zeros((1, BLK), F32))

    tsel = jnp.broadcast_to(jnp.maximum(thr, KEY_MASKED + 1), (GK, BLK))

    qsT = qlatT_ref[0, 0]
    W = DSA_HEADS * BLK
    acc_ref[...] = jnp.zeros_like(acc_ref)

    def logits(g, m_run):
        s = jnp.dot(c_ref[0, rows(g), :], qsT, preferred_element_type=F32)
        sel = key_ref[rows(g), :] >= tsel
        s = jnp.where(jnp.concatenate([sel] * DSA_HEADS, axis=1), s, MASKED)
        s_ref[g % 2] = s
        return jnp.maximum(m_run, jnp.max(s, axis=0, keepdims=True))

    def accumulate(g, m_old, m_new):
        alpha = jnp.exp2(m_old - m_new)
        p = jnp.exp2(s_ref[g % 2] - m_new).astype(BF16)
        acc_ref[...] = alpha * acc_ref[...] + jnp.dot(cT_ref[0, g], p, preferred_element_type=F32)

    m_none = jnp.full((1, W), MASKED, F32)

    def attn_body(g, carry):
        m_old, m_new = carry
        accumulate(g, m_old, m_new)
        return m_new, logits(g + 1, m_new)

    m_old, m_new = lax.fori_loop(0, g_last, attn_body, (m_none, logits(0, m_none)))
    accumulate(g_last, m_old, m_new)
    l = acc_ref[DSA_LATENT:DSA_LATENT + 1, :]
    olatT = acc_ref[0:DSA_LATENT, :] * (1.0 / l)
    out = None
    for h in range(DSA_HEADS):
        olat = olatT[:, h * BLK:(h + 1) * BLK].T.astype(BF16)
        y = jnp.dot(olat, wuv_ref[h], preferred_element_type=F32)
        out = y if out is None else out + y
    o_ref[0] = out.astype(BF16)


def _dsa(qlatT, iqT, iwT, ikw, cT, c, wuv):
    B, S, _ = c.shape
    assert S % GK == 0
    topk = min(DSA_TOPK_MAX, S // 4)
    blk4 = lambda a: pl.BlockSpec((1, 1) + a.shape[2:], lambda b, n: (b, n, 0, 0))
    return pl.pallas_call(
        functools.partial(_dsa_kernel, topk=topk),
        out_shape=jax.ShapeDtypeStruct((B, S, BRANCH_WIDTH), BF16),
        grid=(B, S // BLK),
        in_specs=[blk4(qlatT), blk4(iqT), blk4(iwT),
                  pl.BlockSpec((1,) + ikw.shape[1:], lambda b, n: (b, 0, 0)),
                  pl.BlockSpec((1,) + cT.shape[1:], lambda b, n: (b, 0, 0, 0)),
                  pl.BlockSpec((1,) + c.shape[1:], lambda b, n: (b, 0, 0)),
                  _const_spec(wuv.shape)],
        out_specs=pl.BlockSpec((1, BLK, BRANCH_WIDTH), lambda b, n: (b, n, 0)),
        scratch_shapes=[pltpu.VMEM((S, BLK), I32), pltpu.VMEM((N_CAND, BLK, BLK), I32),
                        pltpu.VMEM((2, GK, DSA_HEADS * BLK), F32),
                        pltpu.VMEM((DSA_LATENT + ONES_ROWS, DSA_HEADS * BLK), F32)],
        compiler_params=_cparams(("arbitrary", "arbitrary")),
        name="dsa",
    )(qlatT, iqT, iwT, ikw, cT, c, wuv)


def _sb_kernel(q_ref, kT_ref, v_ref, o_ref, acc_ref):
    n = pl.program_id(1)
    H = SB_HEADS
    qm = [_head_stack(q_ref[0, i * BLK:(i + 1) * BLK, :], H) for i in range(SB_SUB)]
    row = lax.broadcasted_iota(I32, (BLK, BLK), 0)
    col = lax.broadcasted_iota(I32, (BLK, BLK), 1)
    after = jnp.where(row > col, 1.0, 0.0).astype(BF16)
    after2 = jnp.concatenate([after, after], axis=0)
    strict = jnp.concatenate([col < row] * H, axis=0)

    def step(i, j, run, diag):
        z = jnp.dot(qm[i], kT_ref[0, j], preferred_element_type=F32)
        t = jnp.log(1.0 + jnp.exp(-jnp.abs(z)))
        l1m = -(jnp.maximum(z, 0.0) + t)
        lsg = jnp.minimum(z, 0.0) - t
        if diag:
            l1m = jnp.where(strict, l1m, 0.0)
        hi = l1m.astype(BF16)
        r1 = l1m - hi.astype(F32)
        mid = r1.astype(BF16)
        lo = (r1 - mid.astype(F32)).astype(BF16)
        suf = (jnp.dot(jnp.concatenate([hi, mid], axis=1), after2, preferred_element_type=F32)
               + jnp.dot(lo, after, preferred_element_type=F32))
        a = jnp.exp(lsg + run + suf)
        if diag:
            a = jnp.where(strict, a, 0.0)
        vj = v_ref[0, pl.ds(pl.multiple_of(j * BLK, BLK), BLK), :]
        pv = jnp.dot(a.astype(BF16), vj, preferred_element_type=F32)
        if diag:
            acc_ref[i] = pv
        else:
            acc_ref[i] += pv
        return run + jnp.sum(l1m, axis=1, keepdims=True)

    def any_alive(runs):
        m = jnp.max(runs[0])
        for r in runs[1:]:
            m = jnp.maximum(m, jnp.max(r))
        return m > EXP_ZERO_BELOW

    first = n * SB_SUB
    runs0 = tuple(step(i, first + i, jnp.zeros((H * BLK, 1), F32), True) for i in range(SB_SUB))

    def cond(carry):
        t, _, alive = carry
        return jnp.logical_and(first + SB_SUB - 1 - t >= 0, alive)

    def body(carry):
        t, runs, _ = carry
        new = []
        for i in range(SB_SUB):
            j = first + i - t
            run = jnp.where(j >= 0, runs[i], MASKED)
            new.append(step(i, jnp.maximum(j, 0), run, False))
        return t + 1, tuple(new), any_alive(new)

    lax.while_loop(cond, body, (jnp.int32(1), runs0, any_alive(runs0)))
    for i in range(SB_SUB):
        o_ref[0, i * BLK:(i + 1) * BLK, :] = _head_merge(acc_ref[i], H).astype(BF16)


def _sb(q, kT, v):
    B, S, W = q.shape
    tq = SB_SUB * BLK
    return pl.pallas_call(
        _sb_kernel,
        out_shape=jax.ShapeDtypeStruct((B, S, W), BF16),
        grid=(B, S // tq),
        in_specs=[pl.BlockSpec((1, tq, W), lambda b, n: (b, n, 0)),
                  pl.BlockSpec((1,) + kT.shape[1:], lambda b, n: (b, 0, 0, 0)),
                  pl.BlockSpec((1,) + v.shape[1:], lambda b, n: (b, 0, 0))],
        out_specs=pl.BlockSpec((1, tq, W), lambda b, n: (b, n, 0)),
        scratch_shapes=[pltpu.VMEM((SB_SUB, SB_HEADS * BLK, W), F32)],
        compiler_params=_cparams(("arbitrary", "arbitrary")),
        name="sb",
    )(q, kT, v)


def _swa_kernel(sink_ref, q_ref, kTp_ref, kTc_ref, vp_ref, vc_ref, o_ref):
    n = pl.program_id(1)
    H = SWA_HEADS
    nsub = q_ref.shape[1] // BLK
    r = lax.broadcasted_iota(I32, (BLK, 2 * BLK), 0)
    c = lax.broadcasted_iota(I32, (BLK, 2 * BLK), 1)
    rel = r + BLK - c
    in_window = (rel >= 0) & (rel < SWA_WINDOW)
    sink = jnp.concatenate([jnp.full((BLK, 1), sink_ref[h], F32) for h in range(H)], axis=0)
    for i in range(nsub):
        rows = slice(i * BLK, (i + 1) * BLK)
        qm = _head_stack(q_ref[0, rows, :], H)
        kT_prev = kTp_ref[0, 0] if i == 0 else kTc_ref[0, i - 1]
        v_prev = vp_ref[0] if i == 0 else vc_ref[0, (i - 1) * BLK:i * BLK, :]
        kT = jnp.concatenate([kT_prev, kTc_ref[0, i]], axis=1)
        s = jnp.dot(qm, kT, preferred_element_type=F32)
        ok = in_window & ((c >= BLK) | (n > 0)) if i == 0 else in_window
        s = jnp.where(jnp.concatenate([ok] * H, axis=0), s, MASKED)
        m = jnp.maximum(jnp.max(s, axis=1, keepdims=True), sink)
        p = jnp.exp(s - m)
        denom = jnp.sum(p, axis=1, keepdims=True) + jnp.exp(sink - m)
        p = (p / denom).astype(BF16)
        v = jnp.concatenate([v_prev, vc_ref[0, rows, :]], axis=0)
        o_ref[0, rows, :] = _head_merge(jnp.dot(p, v, preferred_element_type=F32), H).astype(BF16)


def _swa(sinks, q, kT, v, *, tq):
    B, S, W = q.shape
    nsub = tq // BLK
    prev = lambda b, n: (b, jnp.maximum(n * nsub - 1, 0), 0, 0)
    prev3 = lambda b, n: (b, jnp.maximum(n * nsub - 1, 0), 0)
    return pl.pallas_call(
        _swa_kernel,
        out_shape=jax.ShapeDtypeStruct((B, S, W), BF16),
        grid=(B, S // tq),
        in_specs=[pl.BlockSpec(memory_space=pltpu.SMEM),
                  pl.BlockSpec((1, tq, W), lambda b, n: (b, n, 0)),
                  pl.BlockSpec((1, 1, W, BLK), prev),
                  pl.BlockSpec((1, nsub, W, BLK), lambda b, n: (b, n, 0, 0)),
                  pl.BlockSpec((1, BLK, W), prev3),
                  pl.BlockSpec((1, tq, W), lambda b, n: (b, n, 0))],
        out_specs=pl.BlockSpec((1, tq, W), lambda b, n: (b, n, 0)),
        compiler_params=_cparams(("arbitrary", "arbitrary")),
        name="swa",
    )(sinks, q, kT, kT, v, v)


def _merge_kernel(x_ref, pre_ref, wg_ref, oa_ref, ob_ref, oc_ref, od_ref, wb_ref, wo_ref, post_ref, o_ref):
    x = x_ref[...]
    D = x.shape[1]
    hn = _rms(x, pre_ref[...]).astype(BF16)
    merged = None
    for i, br in enumerate((oa_ref, ob_ref, oc_ref, od_ref)):
        gate = jax.nn.sigmoid(jnp.dot(hn, wg_ref[:, i * D:(i + 1) * D], preferred_element_type=F32))
        y = jnp.dot(br[...], wb_ref[i], preferred_element_type=F32)
        merged = gate * y if merged is None else merged + gate * y
    m = jnp.dot(merged.astype(BF16), wo_ref[...], preferred_element_type=F32)
    o_ref[...] = x + _rms(m, post_ref[...])


def _merge(x2, pre, wg, oa, ob, oc, od, wb, wo, post, *, tm):
    T, D = x2.shape
    W = oa.shape[1]
    br = pl.BlockSpec((tm, W), lambda i: (i, 0))
    return pl.pallas_call(
        _merge_kernel,
        out_shape=jax.ShapeDtypeStruct((T, D), F32),
        grid=(T // tm,),
        in_specs=[pl.BlockSpec((tm, D), lambda i: (i, 0)), _const_spec((1, D)), _const_spec(wg.shape),
                  br, br, br, br, _const_spec(wb.shape), _const_spec(wo.shape), _const_spec((1, D))],
        out_specs=pl.BlockSpec((tm, D), lambda i: (i, 0)),
        compiler_params=_cparams(("arbitrary",)),
        name="merge",
    )(x2, pre, wg, oa, ob, oc, od, wb, wo, post)


def _memkv_kernel(mem_ref, g_ref, wkv_ref, kT_ref, v_ref):
    mn = _rms(mem_ref[0], g_ref[...]).astype(BF16)
    kv = jnp.dot(mn, wkv_ref[...], preferred_element_type=F32)
    W = kv.shape[1] // 2
    kT_ref[0] = kv[:, :W].T.astype(BF16)
    v_ref[0] = kv[:, W:].astype(BF16)


def _memkv(mem, g, wkv):
    B, M, D = mem.shape
    W = wkv.shape[1] // 2
    return pl.pallas_call(
        _memkv_kernel,
        out_shape=[jax.ShapeDtypeStruct((B, W, M), BF16), jax.ShapeDtypeStruct((B, M, W), BF16)],
        grid=(B,),
        in_specs=[pl.BlockSpec((1, M, D), lambda b: (b, 0, 0)), _const_spec((1, D)), _const_spec(wkv.shape)],
        out_specs=[pl.BlockSpec((1, W, M), lambda b: (b, 0, 0)), pl.BlockSpec((1, M, W), lambda b: (b, 0, 0))],
        compiler_params=_cparams(("arbitrary",)),
        name="memkv",
    )(mem, g, wkv)


def _xattn_kernel(x_ref, pre_ref, wq_ref, kT_ref, v_ref, wo_ref, post_ref, o_ref):
    x = x_ref[0]
    hn = _rms(x, pre_ref[...]).astype(BF16)
    q = (jnp.dot(hn, wq_ref[...], preferred_element_type=F32) * HEAD_DIM ** -0.5).astype(BF16)
    s = jnp.dot(_head_stack(q, MEM_HEADS), kT_ref[0], preferred_element_type=F32)
    p = jnp.exp(s - jnp.max(s, axis=1, keepdims=True))
    p = (p / jnp.sum(p, axis=1, keepdims=True)).astype(BF16)
    o = _head_merge(jnp.dot(p, v_ref[0], preferred_element_type=F32), MEM_HEADS)
    c = jnp.dot(o.astype(BF16), wo_ref[...], preferred_element_type=F32)
    o_ref[0] = x + _rms(c, post_ref[...])


def _xattn(x, pre, wq, kT, v, wo, post, *, tm):
    B, S, D = x.shape
    return pl.pallas_call(
        _xattn_kernel,
        out_shape=jax.ShapeDtypeStruct((B, S, D), F32),
        grid=(B, S // tm),
        in_specs=[pl.BlockSpec((1, tm, D), lambda b, s: (b, s, 0)), _const_spec((1, D)), _const_spec(wq.shape),
                  pl.BlockSpec((1,) + kT.shape[1:], lambda b, s: (b, 0, 0)),
                  pl.BlockSpec((1,) + v.shape[1:], lambda b, s: (b, 0, 0)),
                  _const_spec(wo.shape), _const_spec((1, D))],
        out_specs=pl.BlockSpec((1, tm, D), lambda b, s: (b, s, 0)),
        compiler_params=_cparams(("arbitrary", "arbitrary")),
        name="xattn",
    )(x, pre, wq, kT, v, wo, post)


def _mixer_weights(w_in, pool_w, dsa_w_uk, dsa_w_uv):
    D = w_in.shape[0]
    sizes = (BRANCH_WIDTH, DSA_HEADS * HEAD_DIM, DSA_LATENT, IDX_HEADS * IDX_DIM, IDX_DIM, IDX_HEADS,
             SB_HEADS * HEAD_DIM, SB_HEADS * HEAD_DIM, SB_HEADS * HEAD_DIM,
             SWA_HEADS * HEAD_DIM, SWA_KV_HEADS * HEAD_DIM, SWA_KV_HEADS * HEAD_DIM)
    cols, acc = [], 0
    for s in sizes:
        cols.append(w_in[:, acc:acc + s])
        acc += s
    (w_pool, w_dq, w_ckv, w_iq, w_ik, w_iw, w_sbq, w_sbk, w_sbv, w_swq, w_swk, w_swv) = cols
    w_gate = w_in[:, acc:]
    group = SWA_HEADS // SWA_KV_HEADS

    def expand_kv(w):
        return jnp.repeat(w.reshape(D, SWA_KV_HEADS, HEAD_DIM), group, axis=1).reshape(D, SWA_HEADS * HEAD_DIM)

    ikw_pad = jnp.zeros((D, 128 - IDX_DIM - IDX_HEADS), w_in.dtype)
    w_mix = jnp.concatenate([w_pool, w_dq, w_ckv, w_iq, w_ik, w_iw, ikw_pad, w_sbq, w_sbk, w_sbv,
                             w_swq, expand_kv(w_swk), expand_kv(w_swv)], axis=1).astype(BF16)
    gd = pool_w.shape[1]
    pool_bd = jnp.zeros((BRANCH_WIDTH, BRANCH_WIDTH), F32)
    for g in range(pool_w.shape[0]):
        pool_bd = pool_bd.at[g * gd:(g + 1) * gd, g * gd:(g + 1) * gd].set(pool_w[g])
    wuk_bd = jnp.zeros((DSA_HEADS * HEAD_DIM, DSA_HEADS * DSA_LATENT), F32)
    wuv_pad = jnp.zeros((DSA_HEADS, DSA_LATENT, DSA_HEADS * HEAD_DIM), F32)
    for h in range(DSA_HEADS):
        wuk_bd = wuk_bd.at[h * HEAD_DIM:(h + 1) * HEAD_DIM, h * DSA_LATENT:(h + 1) * DSA_LATENT].set(dsa_w_uk[:, h, :].T)
        wuv_pad = wuv_pad.at[h, :, h * HEAD_DIM:(h + 1) * HEAD_DIM].set(dsa_w_uv[:, h, :])
    return w_mix, w_gate.astype(BF16), pool_bd.astype(BF16), wuk_bd.astype(BF16), wuv_pad.astype(BF16)


def kernel(x, mem, ffn1_pre, ffn1_w_up, ffn1_w_down, ffn1_post, mix_pre, w_in, pool_w, pool_scale, dsa_kv_norm, dsa_w_uk, dsa_w_uv, swa_sinks, w_branch, w_out, mix_post, xattn_pre, mem_norm, xattn_w_q, xattn_w_kv, xattn_w_o, xattn_post, ffn2_pre, ffn2_w_up, ffn2_w_down, ffn2_post):
    B, S, D = x.shape
    depth = w_in.shape[0]
    T = B * S
    tm_ffn = min(1024, T)
    tf = 256
    tm_proj = min(512, S)
    tm_merge = min(512, T)
    tm_x = min(512, S)
    row = lambda a: a.reshape(1, -1)
    for l in range(depth):
        x2 = _ffn(x.reshape(T, D), row(ffn1_pre[l]), *_ffn_weights(ffn1_w_up[l], ffn1_w_down[l], tf),
                  row(ffn1_post[l]), tm=tm_ffn, tf=tf)
        w_mix, w_gate, pool_bd, wuk_bd, wuv_pad = _mixer_weights(w_in[l], pool_w[l], dsa_w_uk[l], dsa_w_uv[l])
        (oa, qlatT, c, cT, iqT, ikw, iwT, sbq, sbkT, sbv, swq, swkT, swv) = _proj(
            x2.reshape(B, S, D), row(mix_pre[l]), w_mix, pool_bd, row(pool_scale[l]), row(dsa_kv_norm[l]),
            wuk_bd, tm=tm_proj)
        ob = _dsa(qlatT, iqT, iwT, ikw, cT, c, wuv_pad)
        oc = _sb(sbq, sbkT, sbv)
        od = _swa(swa_sinks[l], swq, swkT, swv, tq=min(512, S))
        W = BRANCH_WIDTH
        x2 = _merge(x2, row(mix_pre[l]), w_gate, oa.reshape(T, W), ob.reshape(T, W), oc.reshape(T, W),
                    od.reshape(T, W), w_branch[l].astype(BF16), w_out[l].astype(BF16), row(mix_post[l]),
                    tm=tm_merge)
        mkT, mv = _memkv(mem, row(mem_norm[l]), xattn_w_kv[l].astype(BF16))
        x3 = _xattn(x2.reshape(B, S, D), row(xattn_pre[l]), xattn_w_q[l].astype(BF16), mkT, mv,
                    xattn_w_o[l].astype(BF16), row(xattn_post[l]), tm=tm_x)
        x = _ffn(x3.reshape(T, D), row(ffn2_pre[l]), *_ffn_weights(ffn2_w_up[l], ffn2_w_down[l], tf),
                 row(ffn2_post[l]), tm=tm_ffn, tf=tf).reshape(B, S, D)
    return x
```

```python
import functools

import jax
import jax.numpy as jnp
from jax import lax
from jax.experimental import pallas as pl
from jax.experimental.pallas import tpu as pltpu

F32 = jnp.float32
BF16 = jnp.bfloat16
I32 = jnp.int32

RMS_EPS = 1e-6
HEAD_DIM = 64
BRANCH_WIDTH = 256
N_BRANCH = 4
POOL_WINDOWS = (2, 4, 8, 16)
POOL_HALO = 16
DSA_HEADS = 4
DSA_LATENT = 128
IDX_HEADS = 8
IDX_DIM = 32
DSA_TOPK_MAX = 256
SB_HEADS = 4
SWA_HEADS = 4
SWA_KV_HEADS = 2
SWA_WINDOW = 128
MEM_HEADS = 4

BLK = 128
GRP = 4
GK = GRP * BLK
SB_SUB = 2
VMEM_LIMIT = 48 * 1024 * 1024
MASKED = -1e30
INT_MIN = -(2 ** 31)
COARSE_BITS = 15
N_CAND = 4
CAND_ROWS = 32
REFINE_MAX = 64
ONES_ROWS = 16
SCORE_MASKED = -3.3895313892515355e38
KEY_MASKED = (0xFF7F0000 - (1 << 32)) ^ 0x7FFFFFFF
LOG2E = 1.4426950408889634
EXP_ZERO_BELOW = -104.0


def _cparams(sem):
    return pltpu.CompilerParams(dimension_semantics=sem, vmem_limit_bytes=VMEM_LIMIT)


def _rms(x, g):
    return x * lax.rsqrt(jnp.mean(x * x, axis=-1, keepdims=True) + RMS_EPS) * g


def _const_spec(shape):
    nd = len(shape)
    return pl.BlockSpec(shape, lambda *_: (0,) * nd)


def _loop_pairs(n, body, carry):
    def two(j, c):
        return body(2 * j + 1, 1, body(2 * j, 0, c))

    carry = lax.fori_loop(0, n // 2, two, carry)
    return lax.cond(n % 2 == 1, lambda c: body(n - 1, 0, c), lambda c: c, carry)


def _head_stack(q, heads):
    rows, width = q.shape
    per = width // heads
    lane = lax.broadcasted_iota(I32, (rows, width), 1)
    zero = jnp.zeros_like(q)
    return jnp.concatenate(
        [jnp.where((lane >= h * per) & (lane < (h + 1) * per), q, zero) for h in range(heads)], axis=0)


def _head_merge(o, heads):
    rows = o.shape[0] // heads
    width = o.shape[1]
    per = width // heads
    lane = lax.broadcasted_iota(I32, (rows, width), 1)
    out = jnp.zeros((rows, width), o.dtype)
    for h in range(heads):
        out = jnp.where((lane >= h * per) & (lane < (h + 1) * per), o[h * rows:(h + 1) * rows], out)
    return out


def _ffn_kernel(x_ref, pre_ref, wup_ref, wd_ref, post_ref, o_ref, acc_ref, *, tf):
    x = x_ref[...]
    hn = _rms(x, pre_ref[...]).astype(BF16)
    F = wd_ref.shape[0]
    for j in range(F // tf):
        g = jnp.dot(hn, wup_ref[:, j * tf:(j + 1) * tf], preferred_element_type=F32)
        u = jnp.dot(hn, wup_ref[:, F + j * tf:F + (j + 1) * tf], preferred_element_type=F32)
        a = (g * jax.nn.sigmoid(g) * u).astype(BF16)
        d = jnp.dot(a, wd_ref[j * tf:(j + 1) * tf, :], preferred_element_type=F32)
        if j == 0:
            acc_ref[...] = d
        else:
            acc_ref[...] += d
    o_ref[...] = x + 0.5 * _rms(acc_ref[...], post_ref[...])


def _resident_spec(shape):
    nd = len(shape)
    return pl.BlockSpec(shape, lambda *_: (0,) * nd, pipeline_mode=pl.Buffered(1))


def _ffn(x2, pre, w_up, w_down, post, *, tm, tf):
    T, D = x2.shape
    return pl.pallas_call(
        functools.partial(_ffn_kernel, tf=tf),
        out_shape=jax.ShapeDtypeStruct((T, D), F32),
        grid=(T // tm,),
        in_specs=[
            pl.BlockSpec((tm, D), lambda i: (i, 0)),
            _const_spec((1, D)),
            _resident_spec(w_up.shape),
            _resident_spec(w_down.shape),
            _const_spec((1, D)),
        ],
        out_specs=pl.BlockSpec((tm, D), lambda i: (i, 0)),
        scratch_shapes=[pltpu.VMEM((tm, D), F32)],
        compiler_params=_cparams(("arbitrary",)),
        name="ffn",
    )(x2, pre, w_up, w_down, post)


_SEG = dict(pool=(0, 256), dq=(256, 512), ckv=(512, 640), iq=(640, 896), ikw=(896, 1024),
            sbq=(1024, 1280), sbk=(1280, 1536), sbv=(1536, 1792),
            swq=(1792, 2048), swk=(2048, 2304), swv=(2304, 2560))
_PROJ_W = 2560


def _proj_kernel(x_ref, pre_ref, w_ref, poolw_ref, pools_ref, kvn_ref, wuk_ref,
                 oa_ref, qlatT_ref, c_ref, cT_ref, iqT_ref, ikw_ref, iwT_ref,
                 sbq_ref, sbkT_ref, sbv_ref, swq_ref, swkT_ref, swv_ref, halo_ref):
    si = pl.program_id(1)
    tm = x_ref.shape[1]
    nb = tm // BLK
    hn = _rms(x_ref[0], pre_ref[...]).astype(BF16)
    p = jnp.dot(hn, w_ref[...], preferred_element_type=F32)

    def seg(name):
        a, b = _SEG[name]
        return p[:, a:b]

    @pl.when(si == 0)
    def _():
        halo_ref[...] = jnp.zeros_like(halo_ref)

    u = seg("pool")
    s1 = jnp.concatenate([halo_ref[...], u], axis=0)
    halo_ref[...] = u[tm - POOL_HALO:, :]
    s2 = s1[1:] + s1[:-1]
    s4 = s2[2:] + s2[:-2]
    s8 = s4[4:] + s4[:-4]
    s16 = s8[8:] + s8[:-8]
    lane = lax.broadcasted_iota(I32, (tm, BRANCH_WIDTH), 1)
    t1 = (lax.broadcasted_iota(I32, (tm, BRANCH_WIDTH), 0) + si * tm + 1).astype(F32)
    gw = BRANCH_WIDTH // len(POOL_WINDOWS)
    wsum = jnp.where(lane < gw, s2[-tm:], jnp.where(lane < 2 * gw, s4[-tm:],
                     jnp.where(lane < 3 * gw, s8[-tm:], s16[-tm:])))
    win = jnp.where(lane < gw, float(POOL_WINDOWS[0]), jnp.where(lane < 2 * gw, float(POOL_WINDOWS[1]),
                    jnp.where(lane < 3 * gw, float(POOL_WINDOWS[2]), float(POOL_WINDOWS[3]))))
    pooled = wsum / jnp.minimum(t1, win) - u
    mixed = jnp.dot(pooled.astype(BF16), poolw_ref[...], preferred_element_type=F32)
    oa_ref[0] = (mixed * pools_ref[...]).astype(BF16)

    qlat = jnp.dot(seg("dq").astype(BF16), wuk_ref[...], preferred_element_type=F32) * (HEAD_DIM ** -0.5 * LOG2E)
    c = _rms(seg("ckv"), kvn_ref[...])
    c_ref[0] = c.astype(BF16)
    iq = seg("iq") * IDX_DIM ** -0.5
    ikw = seg("ikw")
    ikw_ref[0] = ikw.astype(BF16)
    sbq_ref[0] = (seg("sbq") * HEAD_DIM ** -0.5).astype(BF16)
    sbv_ref[0] = seg("sbv").astype(BF16)
    swq_ref[0] = (seg("swq") * HEAD_DIM ** -0.5).astype(BF16)
    swv_ref[0] = seg("swv").astype(BF16)
    sbk = seg("sbk")
    swk = seg("swk")
    for g in range(tm // GK):
        cT_ref[0, g, 0:DSA_LATENT, :] = c[g * GK:(g + 1) * GK].T.astype(BF16)
        cT_ref[0, g, DSA_LATENT:, :] = jnp.ones((ONES_ROWS, GK), BF16)
    for b in range(nb):
        rows = slice(b * BLK, (b + 1) * BLK)
        for h in range(DSA_HEADS):
            lanes = slice(h * DSA_LATENT, (h + 1) * DSA_LATENT)
            qlatT_ref[0, b, :, lanes] = qlat[rows, lanes].T.astype(BF16)
        iqT_ref[0, b] = iq[rows].T.astype(BF16)
        iwT_ref[0, b] = ikw[rows].T[IDX_DIM:IDX_DIM + IDX_HEADS] * IDX_HEADS ** -0.5
        sbkT_ref[0, b] = sbk[rows].T.astype(BF16)
        swkT_ref[0, b] = swk[rows].T.astype(BF16)


def _proj(x, pre, w, poolw, pools, kvn, wuk, *, tm):
    B, S, D = x.shape
    nq = S // BLK
    nb = tm // BLK

    def rows(width, dtype):
        return jax.ShapeDtypeStruct((B, S, width), dtype), pl.BlockSpec((1, tm, width), lambda b, s: (b, s, 0))

    def blocks(height, width=BLK, dtype=BF16):
        return (jax.ShapeDtypeStruct((B, nq, height, width), dtype),
                pl.BlockSpec((1, nb, height, width), lambda b, s: (b, s, 0, 0)))

    outs = [rows(256, BF16),
            blocks(DSA_LATENT, DSA_HEADS * BLK),
            rows(128, BF16),
            (jax.ShapeDtypeStruct((B, S // GK, DSA_LATENT + ONES_ROWS, GK), BF16),
             pl.BlockSpec((1, tm // GK, DSA_LATENT + ONES_ROWS, GK), lambda b, s: (b, s, 0, 0))),
            blocks(256),
            rows(128, BF16),
            blocks(IDX_HEADS, BLK, F32),
            rows(256, BF16),
            blocks(256),
            rows(256, BF16),
            rows(256, BF16),
            blocks(256),
            rows(256, BF16)]
    return pl.pallas_call(
        _proj_kernel,
        out_shape=[o[0] for o in outs],
        grid=(B, S // tm),
        in_specs=[pl.BlockSpec((1, tm, D), lambda b, s: (b, s, 0)),
                  _const_spec((1, D)), _const_spec(w.shape), _const_spec(poolw.shape),
                  _const_spec(pools.shape), _const_spec(kvn.shape), _const_spec(wuk.shape)],
        out_specs=[o[1] for o in outs],
        scratch_shapes=[pltpu.VMEM((POOL_HALO, BRANCH_WIDTH), F32)],
        compiler_params=_cparams(("arbitrary", "arbitrary")),
        name="proj",
    )(x, pre, w, poolw, pools, kvn, wuk)


def _key_value(k):
    return pltpu.bitcast(k ^ ((k >> 31) & 0x7FFFFFFF), F32)


def _dsa_kernel(qlatT_ref, iqT_ref, iwT_ref, ikw_ref, cT_ref, c_ref, wuv_ref, o_ref,
                score_ref, cand_ref, s_ref, acc_ref, *, topk):
    n = pl.program_id(1)
    g_last = n // GRP
    ngrp = g_last + 1
    kf = float(topk)
    krow = lax.broadcasted_iota(I32, (BLK, BLK), 0)
    qcol = lax.broadcasted_iota(I32, (BLK, BLK), 1)

    def rows(g):
        return pl.ds(pl.multiple_of(g * GK, GK), GK)

    iqT = iqT_ref[0, 0]
    zpad = jnp.zeros((BLK - IDX_DIM, BLK), BF16)
    iqw = jnp.concatenate([jnp.concatenate([iqT[h * IDX_DIM:(h + 1) * IDX_DIM], zpad], axis=0)
                           for h in range(IDX_HEADS)], axis=1)
    iwT = iwT_ref[0, 0]
    iwb = [jnp.broadcast_to(iwT[h:h + 1, :], (GK, BLK)) for h in range(IDX_HEADS)]

    def score_group(g):
        kj = ikw_ref[0, rows(g), :]
        acc = None
        for hp in range(IDX_HEADS // 2):
            lg = jnp.dot(kj, iqw[:, hp * 2 * BLK:(hp + 1) * 2 * BLK], preferred_element_type=F32)
            for hh in range(2):
                term = iwb[2 * hp + hh] * jnp.maximum(lg[:, hh * BLK:(hh + 1) * BLK], 0.0)
                acc = term if acc is None else acc + term
        return acc

    def score_body(g, parity, carry):
        score_ref[rows(g), :] = score_group(g)
        return carry

    _loop_pairs(g_last, score_body, jnp.int32(0))
    kpos = lax.broadcasted_iota(I32, (GK, BLK), 0) + g_last * GK
    qpos = lax.broadcasted_iota(I32, (GK, BLK), 1) + n * BLK
    score_ref[rows(g_last), :] = jnp.where(kpos <= qpos, score_group(g_last), SCORE_MASKED)

    def count_ge_value(v):
        cb = jnp.broadcast_to(v, (BLK, BLK))

        def body(g, a):
            r0 = pl.multiple_of(g * GK, GK)
            for i in range(GRP):
                a = a + jnp.where(score_ref[pl.ds(r0 + i * BLK, BLK), :] >= cb, 1.0, 0.0)
            return a

        a = lax.fori_loop(0, ngrp, body, jnp.zeros((BLK, BLK), F32))
        return jnp.sum(a, axis=0, keepdims=True)

    def count_ge(cand):
        return count_ge_value(_key_value(cand))

    def bisect(count_fn, top_bit, nbits, carry):
        def step(i, c):
            t, c_t, c_up = c
            cand = t + jnp.left_shift(jnp.int32(1), top_bit - i)
            cnt = count_fn(cand)
            ok = cnt >= kf
            return jnp.where(ok, cand, t), jnp.where(ok, cnt, c_t), jnp.where(ok, c_up, cnt)

        return lax.fori_loop(0, nbits, step, carry)

    total = jnp.full((1, BLK), 1.0, F32) * (ngrp * GK).astype(F32)
    cnt0 = count_ge(jnp.zeros((1, BLK), I32))
    ok0 = cnt0 >= kf
    coarse = bisect(count_ge, 30, COARSE_BITS - 1,
                    (jnp.where(ok0, 0, INT_MIN).astype(I32), jnp.where(ok0, cnt0, total), jnp.where(ok0, 0.0, cnt0)))
    t_c, _, c_up_c = coarse
    fine_bits = 32 - COARSE_BITS

    lo_b = jnp.broadcast_to(_key_value(t_c), (CAND_ROWS, BLK))
    hi_b = jnp.broadcast_to(_key_value(t_c + 2 ** fine_bits), (CAND_ROWS, BLK))
    empty = jnp.full((CAND_ROWS, BLK), -jnp.inf, F32)
    spill = empty
    for p in range(BLK // CAND_ROWS):
        def gather(g, carry, p=p):
            kept, dropped = list(carry[:N_CAND]), carry[N_CAND]
            r0 = pl.multiple_of(g * GK, GK)
            for i in range(GRP):
                x = score_ref[pl.ds(r0 + i * BLK + p * CAND_ROWS, CAND_ROWS), :]
                x = jnp.where(x >= lo_b, x, -jnp.inf)
                x = jnp.where(x < hi_b, x, -jnp.inf)
                for lvl in range(N_CAND):
                    kept[lvl], x = jnp.maximum(kept[lvl], x), jnp.minimum(kept[lvl], x)
                dropped = jnp.maximum(dropped, x)
            return (*kept, dropped)

        res = lax.fori_loop(0, ngrp, gather, (empty,) * N_CAND + (spill,))
        for lvl in range(N_CAND):
            cand_ref[lvl, p * CAND_ROWS:(p + 1) * CAND_ROWS, :] = res[lvl]
        spill = res[N_CAND]

    def count_cand(cand):
        cb = jnp.broadcast_to(_key_value(cand), (BLK, BLK))
        a = jnp.where(cand_ref[0] >= cb, 1.0, 0.0)
        for lvl in range(1, N_CAND):
            a = a + jnp.where(cand_ref[lvl] >= cb, 1.0, 0.0)
        return c_up_c + jnp.sum(a, axis=0, keepdims=True)

    thr, cnt_thr, cnt_gt = lax.cond(jnp.max(spill) > -jnp.inf,
                                    lambda: bisect(count_ge, fine_bits - 1, fine_bits, coarse),
                                    lambda: bisect(count_cand, fine_bits - 1, fine_bits, coarse))

    def refine(carry):
        lo, c_lo, hi, c_hi, it, _ = carry
        mid = 0.5 * (lo + hi)
        use = (c_lo > kf) & (mid > lo) & (mid < hi)
        c = count_ge_value(jnp.where(use, mid, lo))
        up = use & (c >= kf)
        dn = use & (c < kf)
        lo, c_lo = jnp.where(up, mid, lo), jnp.where(up, c, c_lo)
        hi, c_hi = jnp.where(dn, mid, hi), jnp.where(dn, c, c_hi)
        return lo, c_lo, hi, c_hi, it + 1, splittable(lo, c_lo, hi)

    def splittable(lo, c_lo, hi):
        mid = 0.5 * (lo + hi)
        return jnp.max(jnp.where((c_lo > kf) & (mid > lo) & (mid < hi), 1.0, 0.0)) > 0.0

    lo0, hi0 = _key_value(thr), _key_value(thr + 1)
    thr_v, cnt_thr, hi_v, cnt_gt, _, _ = lax.while_loop(
        lambda c: jnp.logical_and(c[4] < REFINE_MAX, c[5]), refine,
        (lo0, cnt_thr, hi0, cnt_gt, jnp.int32(0), splittable(lo0, cnt_thr, hi0)))
    need = kf - cnt_gt
    n_eq = cnt_thr - cnt_gt

    @pl.when(jnp.max(jnp.where(thr > KEY_MASKED, n_eq - need, 0.0)) > 0.0)
    def _():
        tb = jnp.broadcast_to(thr_v, (BLK, BLK))
        hb = jnp.broadcast_to(hi_v, (BLK, BLK))
        incl = jnp.where(qcol <= krow, 1.0, 0.0).astype(BF16)

        def body(j, run):
            r = pl.ds(pl.multiple_of(j * BLK, BLK), BLK)
            k = score_ref[r, :]
            eq = (k >= tb) & (k < hb)
            eqf = jnp.where(eq, 1.0, 0.0)
            pref = jnp.dot(incl, eqf.astype(BF16), preferred_element_type=F32)
            keep = (run + pref) <= need
            score_ref[r, :] = jnp.where(eq & jnp.logical_not(keep), SCORE_MASKED, k)
            return run + jnp.sum(eqf, axis=0, keepdims=True)

        lax.fori_loop(0, n + 1, body, jnp.zeros((1, BLK), F32))

    tsel = jnp.broadcast_to(jnp.maximum(thr_v, _key_value(jnp.full((1, BLK), KEY_MASKED + 1, I32))), (GK, BLK))

    qsT = qlatT_ref[0, 0]
    W = DSA_HEADS * BLK
    acc_ref[...] = jnp.zeros_like(acc_ref)

    def logits(g, slot, m_run):
        s = jnp.dot(c_ref[0, rows(g), :], qsT, preferred_element_type=F32)
        sel = score_ref[rows(g), :] >= tsel
        s = jnp.where(jnp.concatenate([sel] * DSA_HEADS, axis=1), s, MASKED)
        s_ref[slot] = s
        return jnp.maximum(m_run, jnp.max(s, axis=0, keepdims=True))

    def accumulate(g, slot, m_old, m_new):
        alpha = jnp.exp2(m_old - m_new)
        p = jnp.exp2(s_ref[slot] - m_new).astype(BF16)
        acc_ref[...] = alpha * acc_ref[...] + jnp.dot(cT_ref[0, g], p, preferred_element_type=F32)

    m_none = jnp.full((1, W), MASKED, F32)

    def attn_body(g, parity, carry):
        m_old, m_new = carry
        accumulate(g, parity, m_old, m_new)
        return m_new, logits(g + 1, 1 - parity, m_new)

    m_old, m_new = _loop_pairs(g_last, attn_body, (m_none, logits(0, 0, m_none)))
    accumulate(g_last, g_last % 2, m_old, m_new)
    l = acc_ref[DSA_LATENT:DSA_LATENT + 1, :]
    olatT = acc_ref[0:DSA_LATENT, :] * (1.0 / l)
    out = None
    for h in range(DSA_HEADS):
        olat = olatT[:, h * BLK:(h + 1) * BLK].T.astype(BF16)
        y = jnp.dot(olat, wuv_ref[h], preferred_element_type=F32)
        out = y if out is None else out + y
    o_ref[0] = out.astype(BF16)


def _dsa(qlatT, iqT, iwT, ikw, cT, c, wuv):
    B, S, _ = c.shape
    assert S % GK == 0
    topk = min(DSA_TOPK_MAX, S // 4)
    blk4 = lambda a: pl.BlockSpec((1, 1) + a.shape[2:], lambda b, n: (b, n, 0, 0))
    return pl.pallas_call(
        functools.partial(_dsa_kernel, topk=topk),
        out_shape=jax.ShapeDtypeStruct((B, S, BRANCH_WIDTH), BF16),
        grid=(B, S // BLK),
        in_specs=[blk4(qlatT), blk4(iqT), blk4(iwT),
                  pl.BlockSpec((1,) + ikw.shape[1:], lambda b, n: (b, 0, 0)),
                  pl.BlockSpec((1,) + cT.shape[1:], lambda b, n: (b, 0, 0, 0)),
                  pl.BlockSpec((1,) + c.shape[1:], lambda b, n: (b, 0, 0)),
                  _const_spec(wuv.shape)],
        out_specs=pl.BlockSpec((1, BLK, BRANCH_WIDTH), lambda b, n: (b, n, 0)),
        scratch_shapes=[pltpu.VMEM((S, BLK), F32), pltpu.VMEM((N_CAND, BLK, BLK), F32),
                        pltpu.VMEM((2, GK, DSA_HEADS * BLK), F32),
                        pltpu.VMEM((DSA_LATENT + ONES_ROWS, DSA_HEADS * BLK), F32)],
        compiler_params=_cparams(("arbitrary", "arbitrary")),
        name="dsa",
    )(qlatT, iqT, iwT, ikw, cT, c, wuv)


def _sb_kernel(q_ref, kT_ref, v_ref, o_ref, acc_ref):
    n = pl.program_id(1)
    H = SB_HEADS
    qm = [_head_stack(q_ref[0, i * BLK:(i + 1) * BLK, :], H) for i in range(SB_SUB)]
    row = lax.broadcasted_iota(I32, (BLK, BLK), 0)
    col = lax.broadcasted_iota(I32, (BLK, BLK), 1)
    after = jnp.where(row > col, 1.0, 0.0).astype(BF16)
    after2 = jnp.concatenate([after, after], axis=0)
    strict = jnp.concatenate([col < row] * H, axis=0)

    def step(i, j, run, diag):
        z = jnp.dot(qm[i], kT_ref[0, j], preferred_element_type=F32)
        t = jnp.log(1.0 + jnp.exp(-jnp.abs(z)))
        l1m = -(jnp.maximum(z, 0.0) + t)
        lsg = jnp.minimum(z, 0.0) - t
        if diag:
            l1m = jnp.where(strict, l1m, 0.0)
        hi = l1m.astype(BF16)
        r1 = l1m - hi.astype(F32)
        mid = r1.astype(BF16)
        lo = (r1 - mid.astype(F32)).astype(BF16)
        suf = (jnp.dot(jnp.concatenate([hi, mid], axis=1), after2, preferred_element_type=F32)
               + jnp.dot(lo, after, preferred_element_type=F32))
        a = jnp.exp(lsg + run + suf)
        if diag:
            a = jnp.where(strict, a, 0.0)
        vj = v_ref[0, pl.ds(pl.multiple_of(j * BLK, BLK), BLK), :]
        pv = jnp.dot(a.astype(BF16), vj, preferred_element_type=F32)
        if diag:
            acc_ref[i] = pv
        else:
            acc_ref[i] += pv
        return run + jnp.sum(l1m, axis=1, keepdims=True)

    def any_alive(runs):
        m = jnp.max(runs[0])
        for r in runs[1:]:
            m = jnp.maximum(m, jnp.max(r))
        return m > EXP_ZERO_BELOW

    first = n * SB_SUB
    runs0 = tuple(step(i, first + i, jnp.zeros((H * BLK, 1), F32), True) for i in range(SB_SUB))

    def cond(carry):
        t, _, alive = carry
        return jnp.logical_and(first + SB_SUB - 1 - t >= 0, alive)

    def body(carry):
        t, runs, _ = carry
        new = []
        for i in range(SB_SUB):
            j = first + i - t
            run = jnp.where(j >= 0, runs[i], MASKED)
            new.append(step(i, jnp.maximum(j, 0), run, False))
        return t + 1, tuple(new), any_alive(new)

    lax.while_loop(cond, body, (jnp.int32(1), runs0, any_alive(runs0)))
    for i in range(SB_SUB):
        o_ref[0, i * BLK:(i + 1) * BLK, :] = _head_merge(acc_ref[i], H).astype(BF16)


def _sb(q, kT, v):
    B, S, W = q.shape
    tq = SB_SUB * BLK
    return pl.pallas_call(
        _sb_kernel,
        out_shape=jax.ShapeDtypeStruct((B, S, W), BF16),
        grid=(B, S // tq),
        in_specs=[pl.BlockSpec((1, tq, W), lambda b, n: (b, n, 0)),
                  pl.BlockSpec((1,) + kT.shape[1:], lambda b, n: (b, 0, 0, 0)),
                  pl.BlockSpec((1,) + v.shape[1:], lambda b, n: (b, 0, 0))],
        out_specs=pl.BlockSpec((1, tq, W), lambda b, n: (b, n, 0)),
        scratch_shapes=[pltpu.VMEM((SB_SUB, SB_HEADS * BLK, W), F32)],
        compiler_params=_cparams(("arbitrary", "arbitrary")),
        name="sb",
    )(q, kT, v)


def _swa_kernel(sink_ref, q_ref, kTp_ref, kTc_ref, vp_ref, vc_ref, o_ref):
    n = pl.program_id(1)
    H = SWA_HEADS
    nsub = q_ref.shape[1] // BLK
    r = lax.broadcasted_iota(I32, (BLK, 2 * BLK), 0)
    c = lax.broadcasted_iota(I32, (BLK, 2 * BLK), 1)
    rel = r + BLK - c
    in_window = (rel >= 0) & (rel < SWA_WINDOW)
    sink = jnp.concatenate([jnp.full((BLK, 1), sink_ref[h], F32) for h in range(H)], axis=0)
    for i in range(nsub):
        rows = slice(i * BLK, (i + 1) * BLK)
        qm = _head_stack(q_ref[0, rows, :], H)
        kT_prev = kTp_ref[0, 0] if i == 0 else kTc_ref[0, i - 1]
        v_prev = vp_ref[0] if i == 0 else vc_ref[0, (i - 1) * BLK:i * BLK, :]
        kT = jnp.concatenate([kT_prev, kTc_ref[0, i]], axis=1)
        s = jnp.dot(qm, kT, preferred_element_type=F32)
        ok = in_window & ((c >= BLK) | (n > 0)) if i == 0 else in_window
        s = jnp.where(jnp.concatenate([ok] * H, axis=0), s, MASKED)
        m = jnp.maximum(jnp.max(s, axis=1, keepdims=True), sink)
        p = jnp.exp(s - m)
        denom = jnp.sum(p, axis=1, keepdims=True) + jnp.exp(sink - m)
        p = (p / denom).astype(BF16)
        v = jnp.concatenate([v_prev, vc_ref[0, rows, :]], axis=0)
        o_ref[0, rows, :] = _head_merge(jnp.dot(p, v, preferred_element_type=F32), H).astype(BF16)


def _swa(sinks, q, kT, v, *, tq):
    B, S, W = q.shape
    nsub = tq // BLK
    prev = lambda b, n: (b, jnp.maximum(n * nsub - 1, 0), 0, 0)
    prev3 = lambda b, n: (b, jnp.maximum(n * nsub - 1, 0), 0)
    return pl.pallas_call(
        _swa_kernel,
        out_shape=jax.ShapeDtypeStruct((B, S, W), BF16),
        grid=(B, S // tq),
        in_specs=[pl.BlockSpec(memory_space=pltpu.SMEM),
                  pl.BlockSpec((1, tq, W), lambda b, n: (b, n, 0)),
                  pl.BlockSpec((1, 1, W, BLK), prev),
                  pl.BlockSpec((1, nsub, W, BLK), lambda b, n: (b, n, 0, 0)),
                  pl.BlockSpec((1, BLK, W), prev3),
                  pl.BlockSpec((1, tq, W), lambda b, n: (b, n, 0))],
        out_specs=pl.BlockSpec((1, tq, W), lambda b, n: (b, n, 0)),
        compiler_params=_cparams(("arbitrary", "arbitrary")),
        name="swa",
    )(sinks, q, kT, kT, v, v)


def _merge_kernel(x_ref, pre_ref, wg_ref, oa_ref, ob_ref, oc_ref, od_ref, wb_ref, wo_ref, post_ref, o_ref):
    x = x_ref[...]
    D = x.shape[1]
    hn = _rms(x, pre_ref[...]).astype(BF16)
    merged = None
    for i, br in enumerate((oa_ref, ob_ref, oc_ref, od_ref)):
        gate = jax.nn.sigmoid(jnp.dot(hn, wg_ref[:, i * D:(i + 1) * D], preferred_element_type=F32))
        y = jnp.dot(br[...], wb_ref[i], preferred_element_type=F32)
        merged = gate * y if merged is None else merged + gate * y
    m = jnp.dot(merged.astype(BF16), wo_ref[...], preferred_element_type=F32)
    o_ref[...] = x + _rms(m, post_ref[...])


def _merge(x2, pre, wg, oa, ob, oc, od, wb, wo, post, *, tm):
    T, D = x2.shape
    W = oa.shape[1]
    br = pl.BlockSpec((tm, W), lambda i: (i, 0))
    return pl.pallas_call(
        _merge_kernel,
        out_shape=jax.ShapeDtypeStruct((T, D), F32),
        grid=(T // tm,),
        in_specs=[pl.BlockSpec((tm, D), lambda i: (i, 0)), _const_spec((1, D)), _const_spec(wg.shape),
                  br, br, br, br, _const_spec(wb.shape), _const_spec(wo.shape), _const_spec((1, D))],
        out_specs=pl.BlockSpec((tm, D), lambda i: (i, 0)),
        compiler_params=_cparams(("arbitrary",)),
        name="merge",
    )(x2, pre, wg, oa, ob, oc, od, wb, wo, post)


def _memkv_kernel(mem_ref, g_ref, wkv_ref, kT_ref, v_ref):
    mn = _rms(mem_ref[0], g_ref[...]).astype(BF16)
    kv = jnp.dot(mn, wkv_ref[...], preferred_element_type=F32)
    W = kv.shape[1] // 2
    kT_ref[0] = kv[:, :W].T.astype(BF16)
    v_ref[0] = kv[:, W:].astype(BF16)


def _memkv(mem, g, wkv):
    B, M, D = mem.shape
    W = wkv.shape[1] // 2
    return pl.pallas_call(
        _memkv_kernel,
        out_shape=[jax.ShapeDtypeStruct((B, W, M), BF16), jax.ShapeDtypeStruct((B, M, W), BF16)],
        grid=(B,),
        in_specs=[pl.BlockSpec((1, M, D), lambda b: (b, 0, 0)), _const_spec((1, D)), _const_spec(wkv.shape)],
        out_specs=[pl.BlockSpec((1, W, M), lambda b: (b, 0, 0)), pl.BlockSpec((1, M, W), lambda b: (b, 0, 0))],
        compiler_params=_cparams(("arbitrary",)),
        name="memkv",
    )(mem, g, wkv)


def _xattn_kernel(x_ref, pre_ref, wq_ref, kT_ref, v_ref, wo_ref, post_ref, o_ref):
    x = x_ref[0]
    hn = _rms(x, pre_ref[...]).astype(BF16)
    q = (jnp.dot(hn, wq_ref[...], preferred_element_type=F32) * HEAD_DIM ** -0.5).astype(BF16)
    s = jnp.dot(_head_stack(q, MEM_HEADS), kT_ref[0], preferred_element_type=F32)
    p = jnp.exp(s - jnp.max(s, axis=1, keepdims=True))
    p = (p / jnp.sum(p, axis=1, keepdims=True)).astype(BF16)
    o = _head_merge(jnp.dot(p, v_ref[0], preferred_element_type=F32), MEM_HEADS)
    c = jnp.dot(o.astype(BF16), wo_ref[...], preferred_element_type=F32)
    o_ref[0] = x + _rms(c, post_ref[...])


def _xattn(x, pre, wq, kT, v, wo, post, *, tm):
    B, S, D = x.shape
    return pl.pallas_call(
        _xattn_kernel,
        out_shape=jax.ShapeDtypeStruct((B, S, D), F32),
        grid=(B, S // tm),
        in_specs=[pl.BlockSpec((1, tm, D), lambda b, s: (b, s, 0)), _const_spec((1, D)), _const_spec(wq.shape),
                  pl.BlockSpec((1,) + kT.shape[1:], lambda b, s: (b, 0, 0)),
                  pl.BlockSpec((1,) + v.shape[1:], lambda b, s: (b, 0, 0)),
                  _const_spec(wo.shape), _const_spec((1, D))],
        out_specs=pl.BlockSpec((1, tm, D), lambda b, s: (b, s, 0)),
        compiler_params=_cparams(("arbitrary", "arbitrary")),
        name="xattn",
    )(x, pre, wq, kT, v, wo, post)


def _mixer_weights(w_in, pool_w, dsa_w_uk, dsa_w_uv):
    D = w_in.shape[0]
    sizes = (BRANCH_WIDTH, DSA_HEADS * HEAD_DIM, DSA_LATENT, IDX_HEADS * IDX_DIM, IDX_DIM, IDX_HEADS,
             SB_HEADS * HEAD_DIM, SB_HEADS * HEAD_DIM, SB_HEADS * HEAD_DIM,
             SWA_HEADS * HEAD_DIM, SWA_KV_HEADS * HEAD_DIM, SWA_KV_HEADS * HEAD_DIM)
    cols, acc = [], 0
    for s in sizes:
        cols.append(w_in[:, acc:acc + s])
        acc += s
    (w_pool, w_dq, w_ckv, w_iq, w_ik, w_iw, w_sbq, w_sbk, w_sbv, w_swq, w_swk, w_swv) = cols
    w_gate = w_in[:, acc:]
    group = SWA_HEADS // SWA_KV_HEADS

    def expand_kv(w):
        return jnp.repeat(w.reshape(D, SWA_KV_HEADS, HEAD_DIM), group, axis=1).reshape(D, SWA_HEADS * HEAD_DIM)

    ikw_pad = jnp.zeros((D, 128 - IDX_DIM - IDX_HEADS), w_in.dtype)
    w_mix = jnp.concatenate([w_pool, w_dq, w_ckv, w_iq, w_ik, w_iw, ikw_pad, w_sbq, w_sbk, w_sbv,
                             w_swq, expand_kv(w_swk), expand_kv(w_swv)], axis=1).astype(BF16)
    gd = pool_w.shape[1]
    pool_bd = jnp.zeros((BRANCH_WIDTH, BRANCH_WIDTH), F32)
    for g in range(pool_w.shape[0]):
        pool_bd = pool_bd.at[g * gd:(g + 1) * gd, g * gd:(g + 1) * gd].set(pool_w[g])
    wuk_bd = jnp.zeros((DSA_HEADS * HEAD_DIM, DSA_HEADS * DSA_LATENT), F32)
    wuv_pad = jnp.zeros((DSA_HEADS, DSA_LATENT, DSA_HEADS * HEAD_DIM), F32)
    for h in range(DSA_HEADS):
        wuk_bd = wuk_bd.at[h * HEAD_DIM:(h + 1) * HEAD_DIM, h * DSA_LATENT:(h + 1) * DSA_LATENT].set(dsa_w_uk[:, h, :].T)
        wuv_pad = wuv_pad.at[h, :, h * HEAD_DIM:(h + 1) * HEAD_DIM].set(dsa_w_uv[:, h, :])
    return w_mix, w_gate.astype(BF16), pool_bd.astype(BF16), wuk_bd.astype(BF16), wuv_pad.astype(BF16)


def kernel(x, mem, ffn1_pre, ffn1_w_up, ffn1_w_down, ffn1_post, mix_pre, w_in, pool_w, pool_scale, dsa_kv_norm, dsa_w_uk, dsa_w_uv, swa_sinks, w_branch, w_out, mix_post, xattn_pre, mem_norm, xattn_w_q, xattn_w_kv, xattn_w_o, xattn_post, ffn2_pre, ffn2_w_up, ffn2_w_down, ffn2_post):
    B, S, D = x.shape
    depth = w_in.shape[0]
    T = B * S
    tm_ffn = min(1024, T)
    tf = 256
    tm_proj = min(512, S)
    tm_merge = min(512, T)
    tm_x = min(512, S)
    row = lambda a: a.reshape(1, -1)
    for l in range(depth):
        x2 = _ffn(x.reshape(T, D), row(ffn1_pre[l]), ffn1_w_up[l].astype(BF16), ffn1_w_down[l].astype(BF16),
                  row(ffn1_post[l]), tm=tm_ffn, tf=tf)
        w_mix, w_gate, pool_bd, wuk_bd, wuv_pad = _mixer_weights(w_in[l], pool_w[l], dsa_w_uk[l], dsa_w_uv[l])
        (oa, qlatT, c, cT, iqT, ikw, iwT, sbq, sbkT, sbv, swq, swkT, swv) = _proj(
            x2.reshape(B, S, D), row(mix_pre[l]), w_mix, pool_bd, row(pool_scale[l]), row(dsa_kv_norm[l]),
            wuk_bd, tm=tm_proj)
        ob = _dsa(qlatT, iqT, iwT, ikw, cT, c, wuv_pad)
        oc = _sb(sbq, sbkT, sbv)
        od = _swa(swa_sinks[l], swq, swkT, swv, tq=min(512, S))
        W = BRANCH_WIDTH
        x2 = _merge(x2, row(mix_pre[l]), w_gate, oa.reshape(T, W), ob.reshape(T, W), oc.reshape(T, W),
                    od.reshape(T, W), w_branch[l].astype(BF16), w_out[l].astype(BF16), row(mix_post[l]),
                    tm=tm_merge)
        mkT, mv = _memkv(mem, row(mem_norm[l]), xattn_w_kv[l].astype(BF16))
        x3 = _xattn(x2.reshape(B, S, D), row(xattn_pre[l]), xattn_w_q[l].astype(BF16), mkT, mv,
                    xattn_w_o[l].astype(BF16), row(xattn_post[l]), tm=tm_x)
        x = _ffn(x3.reshape(T, D), row(ffn2_pre[l]), ffn2_w_up[l].astype(BF16), ffn2_w_down[l].astype(BF16),
                 row(ffn2_post[l]), tm=tm_ffn, tf=tf).reshape(B, S, D)
    return x
```
